```python
import math
import jax, jax.numpy as jnp
from jax import lax
import numpy as np

D_MODEL = 4096
BATCH = 4
SEQ = 2048
DEPTH = 4
DEC_BATCH = 128
DEC_SEQ = 4
PAST_LEN = 16384
PAGE_SIZE = 128

D_MIX = D_MODEL
W_A = D_MIX // 4
N_POOL = 4
CG_A = W_A // N_POOL
POOL_WINDOWS = (2, 4, 8, 16)
POOL_BUF = max(POOL_WINDOWS) - 1
MLSTM_HEADS = 4
W_B = D_MIX // 2
MLSTM_DV = W_B // MLSTM_HEADS
MLSTM_DK = MLSTM_DV // 2
MLSTM_CHUNK = 64
W_C = D_MIX - W_A - W_B
N_CM = 4
CG_C = W_C // N_CM
CHUNK = 128
D_FF = ((8 * D_MODEL // 3 + 255) // 256) * 256
PLE_DIM = 256
EPS = 1e-6

OFF_A = 0
OFF_Q = OFF_A + W_A
OFF_K = OFF_Q + MLSTM_HEADS * MLSTM_DK
OFF_V = OFF_K + MLSTM_HEADS * MLSTM_DK
OFF_O = OFF_V + W_B
OFF_IG = OFF_O + W_B
OFF_FG = OFF_IG + MLSTM_HEADS
OFF_C = OFF_FG + MLSTM_HEADS
D_IN = OFF_C + 2 * W_C

kernel_name = 'hybrid_pool_mlstm_chunkmlp_step'


def rmsnorm(x, g):
    xf = x.astype(jnp.float32)
    y = xf * lax.rsqrt(jnp.mean(xf * xf, axis=-1, keepdims=True) + EPS)
    return (y * g.astype(jnp.float32)).astype(x.dtype)


def pool_mix(a_ext, pos, w_pool, pool_scale):
    B, n_ext, _ = a_ext.shape
    T = n_ext - POOL_BUF
    f32 = jnp.float32
    cs = jnp.cumsum(jnp.pad(a_ext.astype(f32), ((0, 0), (1, 0), (0, 0))), axis=1)
    cur = cs[:, POOL_BUF + 1:]
    means = []
    for g, w in enumerate(POOL_WINDOWS):
        sl = slice(g * CG_A, (g + 1) * CG_A)
        s = cur[..., sl] - cs[:, POOL_BUF + 1 - w: POOL_BUF + 1 - w + T, sl]
        cnt = jnp.minimum(pos + 1, w).astype(f32)[None, :, None]
        means.append(s / cnt)
    d = jnp.concatenate(means, axis=-1) - a_ext[:, POOL_BUF:].astype(f32)
    y = jnp.einsum('btgc,gcd->btgd', d.reshape(B, T, N_POOL, CG_A), w_pool.astype(f32))
    return (y.reshape(B, T, W_A) * pool_scale.astype(f32)).astype(a_ext.dtype)


def mlstm_mix(q, k, v, i_pre, f_pre, c0, n0, m0):
    B, T, H, _ = q.shape
    L = math.gcd(T, MLSTM_CHUNK)
    nc = T // L
    f32 = jnp.float32

    def chunks(a):
        a = a.astype(f32)
        return jnp.moveaxis(a.reshape((B, nc, L) + a.shape[2:]), 1, 0)

    causal = jnp.tril(jnp.ones((L, L), dtype=bool))

    def step(carry, inp):
        c, n, m = carry
        qc, kc, vc, ic, lfc = inp
        fcum = jnp.cumsum(lfc, axis=1).transpose(0, 2, 1)
        ih = ic.transpose(0, 2, 1)
        dlog = jnp.where(causal, fcum[..., :, None] - fcum[..., None, :] + ih[..., None, :], -jnp.inf)
        m_inter = m[..., None] + fcum
        m_tok = jnp.maximum(m_inter, jnp.max(dlog, axis=-1))
        s = jnp.einsum('bihd,bjhd->bhij', qc, kc) * jnp.exp(dlog - m_tok[..., None])
        w_inter = jnp.exp(m_inter - m_tok)
        num = jnp.einsum('bhij,bjhv->bhiv', s, vc) + w_inter[..., None] * jnp.einsum('bihd,bhdv->bhiv', qc, c)
        den = jnp.sum(s, axis=-1) + w_inter * jnp.einsum('bihd,bhd->bhi', qc, n)
        den = jnp.maximum(jnp.abs(den), jnp.exp(-m_tok))
        h = (num / den[..., None]).transpose(0, 2, 1, 3)
        f_tot = fcum[..., -1]
        wlog = f_tot[..., None] - fcum + ih
        m_new = jnp.maximum(m + f_tot, jnp.max(wlog, axis=-1))
        wk = jnp.exp(wlog - m_new[..., None])
        decay = jnp.exp(m + f_tot - m_new)
        kw = kc * wk.transpose(0, 2, 1)[..., None]
        c_new = decay[..., None, None] * c + jnp.einsum('bjhd,bjhv->bhdv', kw, vc)
        n_new = decay[..., None] * n + jnp.sum(kw, axis=1)
        return (c_new, n_new, m_new), h

    log_f = jax.nn.log_sigmoid(f_pre.astype(f32))
    carry, hs = lax.scan(step, (c0.astype(f32), n0.astype(f32), m0.astype(f32)),
                         (chunks(q), chunks(k), chunks(v), chunks(i_pre), chunks(log_f)))
    h = jnp.moveaxis(hs, 0, 1).reshape(B, T, H, MLSTM_DV)
    return h, carry


def chunk_mlp_mix(c, w_s, b_s, g_v):
    B, T, _ = c.shape
    c = jax.nn.gelu(c)
    u, v = c[..., :W_C], c[..., W_C:]
    v = rmsnorm(v.reshape(B, T, N_CM, CG_C), g_v.reshape(N_CM, CG_C))
    tp = -(-T // CHUNK) * CHUNK
    vp = jnp.pad(v, ((0, 0), (0, tp - T), (0, 0), (0, 0))).reshape(B, tp // CHUNK, CHUNK, N_CM, CG_C)
    ws = w_s * jnp.tril(jnp.ones((CHUNK, CHUNK), w_s.dtype))
    mix = jnp.einsum('gij,bnjgc->bnigc', ws, vp) + b_s.T[:, :, None]
    mix = mix.reshape(B, tp, W_C)[:, :T]
    return u * mix, v.reshape(B, T, W_C)


def decoder_layer(x, p, pos, pool_buf, c0, n0, m0, g_mix, w_in, b_i, b_f, w_pool, pool_scale,
                  g_mlstm, w_s, b_s, g_v, w_out, g_ffn, w_fg, w_fu, w_fd, g_ple, w_pg, w_pp):
    B, T, _ = x.shape
    z = rmsnorm(x, g_mix) @ w_in
    a_ext = jnp.concatenate([pool_buf.astype(x.dtype), z[..., OFF_A:OFF_Q]], axis=1)
    y_a = pool_mix(a_ext, pos, w_pool, pool_scale)
    q = z[..., OFF_Q:OFF_K].reshape(B, T, MLSTM_HEADS, MLSTM_DK) * (MLSTM_DK ** -0.5)
    k = z[..., OFF_K:OFF_V].reshape(B, T, MLSTM_HEADS, MLSTM_DK)
    v = z[..., OFF_V:OFF_O].reshape(B, T, MLSTM_HEADS, MLSTM_DV)
    o = z[..., OFF_O:OFF_IG]
    i_pre = z[..., OFF_IG:OFF_FG] + b_i
    f_pre = z[..., OFF_FG:OFF_C] + b_f
    h, (c1, n1, m1) = mlstm_mix(q, k, v, i_pre, f_pre, c0, n0, m0)
    y_b = rmsnorm(h, g_mlstm.reshape(MLSTM_HEADS, MLSTM_DV)).reshape(B, T, W_B).astype(x.dtype) * jax.nn.sigmoid(o)
    y_c, v_rows = chunk_mlp_mix(z[..., OFF_C:], w_s, b_s, g_v)
    x = x + jnp.concatenate([y_a, y_b, y_c], axis=-1) @ w_out
    hf = rmsnorm(x, g_ffn)
    x = x + (jax.nn.silu(hf @ w_fg) * (hf @ w_fu)) @ w_fd
    gate = jax.nn.sigmoid(rmsnorm(x, g_ple) @ w_pg)
    x = x + gate * (p @ w_pp)
    return x, a_ext[:, -POOL_BUF:], c1, n1, m1, v_rows


def setup_inputs(seed: int = 0) -> dict:
    key = jax.random.key(seed)
    ks = jax.random.split(key, 32)
    f32 = jnp.float32

    def nrm(k, shape, scale=1.0):
        return jax.random.normal(k, shape, f32) * scale

    def gain(k, shape):
        return 1.0 + 0.05 * jax.random.normal(k, shape, f32)

    D = D_MODEL
    return {
        'x_prompt': nrm(ks[0], (BATCH, SEQ, D)),
        'x_sample': nrm(ks[1], (DEC_BATCH, DEC_SEQ, D)),
        'state_pool': nrm(ks[2], (DEPTH, DEC_BATCH, POOL_BUF, W_A)),
        'state_mlstm_c': nrm(ks[3], (DEPTH, DEC_BATCH, MLSTM_HEADS, MLSTM_DK, MLSTM_DV), 0.5),
        'state_mlstm_n': nrm(ks[4], (DEPTH, DEC_BATCH, MLSTM_HEADS, MLSTM_DK), 0.5),
        'state_mlstm_m': nrm(ks[5], (DEPTH, DEC_BATCH, MLSTM_HEADS)),
        'p_prompt': nrm(ks[6], (DEPTH, BATCH, SEQ, PLE_DIM)),
        'p_sample': nrm(ks[7], (DEPTH, DEC_BATCH, DEC_SEQ, PLE_DIM)),
        'g_mix': gain(ks[8], (DEPTH, D)),
        'w_in': nrm(ks[9], (DEPTH, D, D_IN), D ** -0.5),
        'b_igate': nrm(ks[10], (DEPTH, MLSTM_HEADS), 0.1),
        'b_fgate': 3.0 + nrm(ks[11], (DEPTH, MLSTM_HEADS), 0.1),
        'w_pool': nrm(ks[12], (DEPTH, N_POOL, CG_A, CG_A), CG_A ** -0.5),
        'pool_scale': gain(ks[13], (DEPTH, W_A)),
        'g_mlstm': gain(ks[14], (DEPTH, W_B)),
        'w_s': nrm(ks[15], (DEPTH, N_CM, CHUNK, CHUNK), CHUNK ** -0.5),
        'b_s': gain(ks[16], (DEPTH, N_CM, CHUNK)),
        'g_v': gain(ks[17], (DEPTH, W_C)),
        'w_out': nrm(ks[18], (DEPTH, D_MIX, D), D_MIX ** -0.5),
        'g_ffn': gain(ks[19], (DEPTH, D)),
        'w_ffn_gate': nrm(ks[20], (DEPTH, D, D_FF), D ** -0.5),
        'w_ffn_up': nrm(ks[21], (DEPTH, D, D_FF), D ** -0.5),
        'w_ffn_down': nrm(ks[22], (DEPTH, D_FF, D), D_FF ** -0.5),
        'g_ple': gain(ks[23], (DEPTH, D)),
        'w_ple_gate': nrm(ks[24], (DEPTH, D, D), D ** -0.5),
        'w_ple_proj': nrm(ks[25], (DEPTH, PLE_DIM, D), PLE_DIM ** -0.5),
        'g_final': gain(ks[26], (D,)),
    }


def reference(x_prompt, x_sample, state_pool, state_mlstm_c, state_mlstm_n, state_mlstm_m,
              p_prompt, p_sample, g_mix, w_in, b_igate, b_fgate, w_pool, pool_scale, g_mlstm,
              w_s, b_s, g_v, w_out, g_ffn, w_ffn_gate, w_ffn_up, w_ffn_down, g_ple,
              w_ple_gate, w_ple_proj, g_final):
    f32 = jnp.float32
    pos_prompt = jnp.arange(SEQ, dtype=jnp.int32)
    pos_sample = PAST_LEN + jnp.arange(DEC_SEQ, dtype=jnp.int32)
    v_start = ((SEQ - 1) // CHUNK) * CHUNK
    buf0 = jnp.zeros((BATCH, POOL_BUF, W_A), x_prompt.dtype)
    c0 = jnp.zeros((BATCH, MLSTM_HEADS, MLSTM_DK, MLSTM_DV), f32)
    n0 = jnp.zeros((BATCH, MLSTM_HEADS, MLSTM_DK), f32)
    m0 = jnp.zeros((BATCH, MLSTM_HEADS), f32)
    xp, xs = x_prompt, x_sample
    pool_p, pool_s, c_p, c_s, n_p, n_s, m_p, m_s, v_p, v_s = ([] for _ in range(10))
    for i in range(DEPTH):
        lw = (g_mix[i], w_in[i], b_igate[i], b_fgate[i], w_pool[i], pool_scale[i], g_mlstm[i],
              w_s[i], b_s[i], g_v[i], w_out[i], g_ffn[i], w_ffn_gate[i], w_ffn_up[i],
              w_ffn_down[i], g_ple[i], w_ple_gate[i], w_ple_proj[i])
        xp, bp, cp, np_, mp, vp = decoder_layer(xp, p_prompt[i], pos_prompt, buf0, c0, n0, m0, *lw)
        xs, bs, cs, ns, ms, vs = decoder_layer(xs, p_sample[i], pos_sample, state_pool[i],
                                               state_mlstm_c[i], state_mlstm_n[i], state_mlstm_m[i], *lw)
        pool_p.append(bp); pool_s.append(bs)
        c_p.append(cp); c_s.append(cs)
        n_p.append(np_); n_s.append(ns)
        m_p.append(mp); m_s.append(ms)
        v_p.append(vp[:, v_start:]); v_s.append(vs)
    y_prompt = rmsnorm(xp, g_final)
    y_sample = rmsnorm(xs, g_final)
    pool_prompt = jnp.stack(pool_p)
    pool_sample = jnp.stack(pool_s)
    c_prompt = jnp.stack(c_p)
    c_sample = jnp.stack(c_s)
    n_prompt = jnp.stack(n_p)
    n_sample = jnp.stack(n_s)
    m_prompt = jnp.stack(m_p)
    m_sample = jnp.stack(m_s)
    v_prompt = jnp.stack(v_p)
    v_sample = jnp.stack(v_s)
    return (y_prompt, y_sample, pool_prompt, pool_sample, c_prompt, c_sample, n_prompt, n_sample, m_prompt, m_sample, v_prompt, v_sample)
```

```python
import functools
import math

import jax
import jax.numpy as jnp
from jax import lax
from jax.experimental import pallas as pl
from jax.experimental.pallas import tpu as pltpu

F32 = jnp.float32
BF16 = jnp.bfloat16

EPS = 1e-6
PAST_LEN = 16384
POOL_WINDOWS = (2, 4, 8, 16)
POOL_BUF = max(POOL_WINDOWS) - 1
POOL_HALO = 16
N_POOL = len(POOL_WINDOWS)
MLSTM_HEADS = 4
N_CM = 4
CHUNK = 128
GATE_LANES = 128

VMEM_SLACK_BYTES = 6 * 2**20
VMEM_CAP_BYTES = 58 * 2**20


def _params(sem, vmem_estimate):
    limit = min(int(vmem_estimate) + VMEM_SLACK_BYTES, VMEM_CAP_BYTES)
    return pltpu.CompilerParams(dimension_semantics=sem, vmem_limit_bytes=limit)


def _pick(n, candidates):
    for c in candidates:
        if n % c == 0:
            return c
    return n


def _single(shape, index_map):
    return pl.BlockSpec(shape, index_map, pipeline_mode=pl.Buffered(1))


def _dot(a, b):
    return jnp.dot(a, b, preferred_element_type=F32)


def _gelu_tanh(x):
    return 0.5 * x * (1.0 + jnp.tanh(math.sqrt(2.0 / math.pi) * (x + 0.044715 * (x * x * x))))


def _seq_of(idx, seq_rows):
    if seq_rows & (seq_rows - 1) == 0:
        return lax.shift_right_logical(idx, seq_rows.bit_length() - 1)
    return idx // seq_rows


def _log_sigmoid(x):
    return jnp.minimum(x, 0.0) - jnp.log(1.0 + jnp.exp(-jnp.abs(x)))


def _rms_rows_to(x_ref, g_ref, h_ref, rows_per_step=32):
    bm = x_ref.shape[0]
    rb = _pick(bm, (rows_per_step, 16, 8))

    def body(r, carry):
        sl = pl.ds(pl.multiple_of(r * rb, rb), rb)
        x = x_ref[sl, :]
        ms = jnp.mean(x * x, axis=-1, keepdims=True)
        h_ref[sl, :] = ((x * lax.rsqrt(ms + EPS)) * g_ref[...]).astype(h_ref.dtype)
        return carry

    lax.fori_loop(0, bm // rb, body, 0)


def _inproj_kernel(x_ref, g_ref, w_ref, wg_ref, z_ref, zg_ref, h_scr):
    @pl.when(pl.program_id(1) == 0)
    def _():
        _rms_rows_to(x_ref, g_ref, h_scr)
        zg_ref[...] = _dot(h_scr[...], wg_ref[...])

    z_ref[...] = _dot(h_scr[...], w_ref[...])


def _inproj(x, g, w_main, w_gate):
    m, d = x.shape
    n = w_main.shape[1]
    bm = _pick(m, (1024, 512, 256, 128, 64, 32, 16, 8))
    bn = _pick(n, (512, 256, 128))
    est = (bm * d * 4 + bm * d * 2 + 2 * d * bn * 2 + 2 * d * GATE_LANES * 2
           + 2 * bm * bn * 4 + 2 * bm * GATE_LANES * 4)
    return pl.pallas_call(
        _inproj_kernel,
        grid=(m // bm, n // bn),
        in_specs=[
            _single((bm, d), lambda i, j: (i, 0)),
            pl.BlockSpec((1, d), lambda i, j: (0, 0)),
            pl.BlockSpec((d, bn), lambda i, j: (0, j)),
            pl.BlockSpec((d, GATE_LANES), lambda i, j: (0, 0)),
        ],
        out_specs=[
            pl.BlockSpec((bm, bn), lambda i, j: (i, j)),
            pl.BlockSpec((bm, GATE_LANES), lambda i, j: (i, 0)),
        ],
        out_shape=[jax.ShapeDtypeStruct((m, n), F32), jax.ShapeDtypeStruct((m, GATE_LANES), F32)],
        scratch_shapes=[pltpu.VMEM((bm, d), BF16)],
        compiler_params=_params(("parallel", "arbitrary"), est),
    )(x, g, w_main, w_gate)


def _ffn_up_kernel(x_ref, g_ref, w_ref, o_ref, h_scr):
    @pl.when(pl.program_id(1) == 0)
    def _():
        _rms_rows_to(x_ref, g_ref, h_scr)

    acc = _dot(h_scr[...], w_ref[...])
    tb = acc.shape[1] // 2
    a = acc[:, :tb]
    b = acc[:, tb:]
    o_ref[...] = ((a * jax.nn.sigmoid(a)) * b).astype(o_ref.dtype)


def _ffn_up(x, g, w_gu, tb):
    m, d = x.shape
    n = w_gu.shape[1] // 2
    bm = _pick(m, (1024, 512, 256, 128, 64, 32, 16, 8))
    est = bm * d * 4 + bm * d * 2 + 2 * d * 2 * tb * 2 + 2 * bm * tb * 2
    return pl.pallas_call(
        _ffn_up_kernel,
        grid=(m // bm, n // tb),
        in_specs=[
            _single((bm, d), lambda i, j: (i, 0)),
            pl.BlockSpec((1, d), lambda i, j: (0, 0)),
            pl.BlockSpec((d, 2 * tb), lambda i, j: (0, j)),
        ],
        out_specs=pl.BlockSpec((bm, tb), lambda i, j: (i, j)),
        out_shape=jax.ShapeDtypeStruct((m, n), BF16),
        scratch_shapes=[pltpu.VMEM((bm, d), BF16)],
        compiler_params=_params(("parallel", "arbitrary"), est),
    )(x, g, w_gu)


def _ple_kernel(x_ref, g_ref, w_ref, p_ref, wpp_ref, o_ref, h_scr):
    j = pl.program_id(1)

    @pl.when(j == 0)
    def _():
        _rms_rows_to(x_ref, g_ref, h_scr)

    bn = o_ref.shape[1]
    gate = jax.nn.sigmoid(_dot(h_scr[...], w_ref[...]))
    emb = _dot(p_ref[...].astype(BF16), wpp_ref[...])
    xres = x_ref[:, pl.ds(pl.multiple_of(j * bn, bn), bn)]
    o_ref[...] = xres + gate * emb


def _ple(x, g, w_pg, p, w_pp):
    m, d = x.shape
    pd = p.shape[1]
    bm = _pick(m, (1024, 512, 256, 128, 64, 32, 16, 8))
    bn = _pick(d, (512, 256, 128))
    est = bm * d * 4 + bm * d * 2 + 2 * d * bn * 2 + 2 * bm * pd * 4 + 2 * pd * bn * 2 + 2 * bm * bn * 4
    return pl.pallas_call(
        _ple_kernel,
        grid=(m // bm, d // bn),
        in_specs=[
            _single((bm, d), lambda i, j: (i, 0)),
            pl.BlockSpec((1, d), lambda i, j: (0, 0)),
            pl.BlockSpec((d, bn), lambda i, j: (0, j)),
            pl.BlockSpec((bm, pd), lambda i, j: (i, 0)),
            pl.BlockSpec((pd, bn), lambda i, j: (0, j)),
        ],
        out_specs=pl.BlockSpec((bm, bn), lambda i, j: (i, j)),
        out_shape=jax.ShapeDtypeStruct((m, d), F32),
        scratch_shapes=[pltpu.VMEM((bm, d), BF16)],
        compiler_params=_params(("parallel", "arbitrary"), est),
    )(x, g, w_pg, p, w_pp)


def _wout_kernel(ya_ref, yb_ref, yc_ref, w_ref, x_ref, o_ref, a_scr):
    @pl.when(pl.program_id(1) == 0)
    def _():
        wa = ya_ref.shape[1]
        wb = yb_ref.shape[1]
        a_scr[:, :wa] = ya_ref[...].astype(a_scr.dtype)
        a_scr[:, wa:wa + wb] = yb_ref[...].astype(a_scr.dtype)
        a_scr[:, wa + wb:] = yc_ref[...].astype(a_scr.dtype)

    o_ref[...] = x_ref[...] + _dot(a_scr[...], w_ref[...])


def _wout(ya, yb, yc, w, x):
    m, d = x.shape
    wa, wb, wc = ya.shape[1], yb.shape[1], yc.shape[1]
    bm = _pick(m, (1024, 512, 256, 128, 64, 32, 16, 8))
    bn = _pick(d, (512, 256, 128))
    est = (bm * wa * ya.dtype.itemsize + bm * wb * 2 + bm * wc * 2 + bm * d * 2
           + 2 * d * bn * 2 + 4 * bm * bn * 4)
    return pl.pallas_call(
        _wout_kernel,
        grid=(m // bm, d // bn),
        in_specs=[
            _single((bm, wa), lambda i, j: (i, 0)),
            _single((bm, wb), lambda i, j: (i, 0)),
            _single((bm, wc), lambda i, j: (i, 0)),
            pl.BlockSpec((wa + wb + wc, bn), lambda i, j: (0, j)),
            pl.BlockSpec((bm, bn), lambda i, j: (i, j)),
        ],
        out_specs=pl.BlockSpec((bm, bn), lambda i, j: (i, j)),
        out_shape=jax.ShapeDtypeStruct((m, d), F32),
        scratch_shapes=[pltpu.VMEM((bm, wa + wb + wc), BF16)],
        compiler_params=_params(("parallel", "arbitrary"), est),
    )(ya, yb, yc, w, x)


def _ffn_down_kernel(a_ref, w_ref, x_ref, o_ref):
    o_ref[...] = x_ref[...] + _dot(a_ref[...], w_ref[...])


def _ffn_down(a, w, x):
    m, d = x.shape
    k = a.shape[1]
    bm = _pick(m, (1024, 512, 256, 128, 64, 32, 16, 8))
    bn = _pick(d, (256, 128))
    est = bm * k * 2 + 2 * k * bn * 2 + 4 * bm * bn * 4
    return pl.pallas_call(
        _ffn_down_kernel,
        grid=(m // bm, d // bn),
        in_specs=[
            _single((bm, k), lambda i, j: (i, 0)),
            pl.BlockSpec((k, bn), lambda i, j: (0, j)),
            pl.BlockSpec((bm, bn), lambda i, j: (i, j)),
        ],
        out_specs=pl.BlockSpec((bm, bn), lambda i, j: (i, j)),
        out_shape=jax.ShapeDtypeStruct((m, d), F32),
        compiler_params=_params(("parallel", "arbitrary"), est),
    )(a, w, x)


def _final_norm_kernel(x_ref, g_ref, o_ref):
    x = x_ref[...]
    ms = jnp.mean(x * x, axis=-1, keepdims=True)
    o_ref[...] = (x * lax.rsqrt(ms + EPS)) * g_ref[...]


def _final_norm(x, g):
    m, d = x.shape
    bm = _pick(m, (256, 128, 64, 32, 16, 8))
    return pl.pallas_call(
        _final_norm_kernel,
        grid=(m // bm,),
        in_specs=[pl.BlockSpec((bm, d), lambda i: (i, 0)), pl.BlockSpec((1, d), lambda i: (0, 0))],
        out_specs=pl.BlockSpec((bm, d), lambda i: (i, 0)),
        out_shape=jax.ShapeDtypeStruct((m, d), F32),
        compiler_params=_params(("parallel",), 4 * bm * d * 4),
    )(x, g)


def _pool_prompt_kernel(a_ref, halo_ref, wp_ref, sc_ref, y_ref, ext_scr, *, tiles_per_seq):
    tp = a_ref.shape[0]
    cg = wp_ref.shape[1]
    tile = pl.program_id(0) % tiles_per_seq
    ext_scr[0:POOL_HALO, :] = jnp.where(tile == 0, 0.0, halo_ref[...])
    ext_scr[POOL_HALO:, :] = a_ref[...]
    pos = tile * tp + lax.broadcasted_iota(jnp.int32, (tp, 1), 0)
    for g, w in enumerate(POOL_WINDOWS):
        sl = slice(g * cg, (g + 1) * cg)
        s = ext_scr[POOL_HALO:POOL_HALO + tp, sl]
        for k in range(1, w):
            s = s + ext_scr[POOL_HALO - k:POOL_HALO - k + tp, sl]
        cnt = jnp.minimum(pos + 1, w).astype(F32)
        dlt = s / cnt - a_ref[:, sl]
        y = _dot(dlt.astype(BF16), wp_ref[g]) * sc_ref[:, sl]
        y_ref[:, sl] = y.astype(y_ref.dtype)


def _pool_prompt(z, seq, w_pool, scale):
    m = z.shape[0]
    wa = scale.shape[1]
    tp = _pick(seq, (512, 256, 128, 64, 32, 16))
    tiles_per_seq = seq // tp
    halo_blocks = tp // POOL_HALO
    est = 2 * tp * wa * 4 + 2 * POOL_HALO * wa * 4 + (tp + POOL_HALO) * wa * 4 + 2 * tp * wa * 2 + 2 * wa * wa // N_POOL * 2
    return pl.pallas_call(
        functools.partial(_pool_prompt_kernel, tiles_per_seq=tiles_per_seq),
        grid=(m // tp,),
        in_specs=[
            pl.BlockSpec((tp, wa), lambda i: (i, 0)),
            pl.BlockSpec((POOL_HALO, wa), lambda i: (jnp.maximum(i * halo_blocks - 1, 0), 0)),
            pl.BlockSpec(w_pool.shape, lambda i: (0, 0, 0)),
            pl.BlockSpec((1, wa), lambda i: (0, 0)),
        ],
        out_specs=pl.BlockSpec((tp, wa), lambda i: (i, 0)),
        out_shape=jax.ShapeDtypeStruct((m, wa), BF16),
        scratch_shapes=[pltpu.VMEM((tp + POOL_HALO, wa), F32)],
        compiler_params=_params(("parallel",), est),
    )(z, z, w_pool, scale)


def _pool_sample_kernel(a_ref, wp_ref, sc_ref, y_ref, *, first_pos):
    cg = wp_ref.shape[1]
    n_new = a_ref.shape[1] - POOL_BUF
    rows = [a_ref[:, r, :] for r in range(a_ref.shape[1])]
    for t in range(n_new):
        for g, w in enumerate(POOL_WINDOWS):
            sl = slice(g * cg, (g + 1) * cg)
            s = rows[POOL_BUF + t][:, sl]
            for k in range(1, w):
                s = s + rows[POOL_BUF + t - k][:, sl]
            cnt = float(min(first_pos + t + 1, w))
            dlt = s / cnt - rows[POOL_BUF + t][:, sl]
            y = _dot(dlt.astype(BF16), wp_ref[g]) * sc_ref[:, sl]
            y_ref[:, t, sl] = y.astype(y_ref.dtype)


def _pool_sample(a_ext, w_pool, scale, first_pos):
    b, n_ext, wa = a_ext.shape
    n_new = n_ext - POOL_BUF
    bb = _pick(b, (32, 16, 8))
    est = 2 * bb * 24 * wa * 4 + 2 * bb * 8 * wa * 4 + 2 * wa * wa // N_POOL * 2
    return pl.pallas_call(
        functools.partial(_pool_sample_kernel, first_pos=first_pos),
        grid=(b // bb,),
        in_specs=[
            pl.BlockSpec((bb, n_ext, wa), lambda i: (i, 0, 0)),
            pl.BlockSpec(w_pool.shape, lambda i: (0, 0, 0)),
            pl.BlockSpec((1, wa), lambda i: (0, 0)),
        ],
        out_specs=pl.BlockSpec((bb, n_new, wa), lambda i: (i, 0, 0)),
        out_shape=jax.ShapeDtypeStruct((b, n_new, wa), F32),
        compiler_params=_params(("parallel",), est),
    )(a_ext, w_pool, scale)


def _chunk_mlp_kernel(u_ref, v_ref, ws_ref, b_ref, gv_ref, yc_ref, vn_ref, *, seq_rows):
    rows = u_ref.shape[0]
    cg = u_ref.shape[1] // N_CM
    ri = lax.broadcasted_iota(jnp.int32, (rows, rows), 0)
    ci = lax.broadcasted_iota(jnp.int32, (rows, rows), 1)
    mask = ci <= ri
    if seq_rows < rows:
        mask = mask & (_seq_of(ri, seq_rows) == _seq_of(ci, seq_rows))
    for g in range(N_CM):
        sl = slice(g * cg, (g + 1) * cg)
        u = _gelu_tanh(u_ref[:, sl])
        v = _gelu_tanh(v_ref[:, sl])
        ms = jnp.mean(v * v, axis=-1, keepdims=True)
        vn = (v * lax.rsqrt(ms + EPS)) * gv_ref[:, sl]
        vn_ref[:, sl] = vn
        wm = jnp.where(mask, ws_ref[g], 0.0).astype(BF16)
        mix = _dot(wm, vn.astype(BF16)) + b_ref[:, g:g + 1]
        yc_ref[:, sl] = (u * mix).astype(yc_ref.dtype)


def _chunk_mlp(z, u_blk, w_mix, bias, g_v, seq_rows):
    m = z.shape[0]
    wc = g_v.shape[1]
    est = 4 * CHUNK * wc * 4 + 2 * N_CM * CHUNK * CHUNK * 4 + 2 * CHUNK * wc * 2 + 2 * CHUNK * wc * 4
    return pl.pallas_call(
        functools.partial(_chunk_mlp_kernel, seq_rows=seq_rows),
        grid=(m // CHUNK,),
        in_specs=[
            pl.BlockSpec((CHUNK, wc), lambda i: (i, u_blk)),
            pl.BlockSpec((CHUNK, wc), lambda i: (i, u_blk + 1)),
            pl.BlockSpec((N_CM, CHUNK, CHUNK), lambda i: (0, 0, 0)),
            pl.BlockSpec((CHUNK, N_CM), lambda i: (0, 0)),
            pl.BlockSpec((1, wc), lambda i: (0, 0)),
        ],
        out_specs=[pl.BlockSpec((CHUNK, wc), lambda i: (i, 0)), pl.BlockSpec((CHUNK, wc), lambda i: (i, 0))],
        out_shape=[jax.ShapeDtypeStruct((m, wc), BF16), jax.ShapeDtypeStruct((m, wc), F32)],
        compiler_params=_params(("parallel",), est),
    )(z, z, w_mix, bias, g_v)


def _col_to_row(col, eye):
    return jnp.sum(jnp.where(eye, col, 0.0), axis=0, keepdims=True)


def _head_blocks(refs, h, width):
    r = refs[h // 2]
    return r[:, (h % 2) * width:(h % 2 + 1) * width]


def _mlstm_gates(zg_ref, bias_ref, h, same, causal, eye):
    i_col = zg_ref[:, h:h + 1] + bias_ref[:, h:h + 1]
    f_col = zg_ref[:, MLSTM_HEADS + h:MLSTM_HEADS + h + 1] + bias_ref[:, MLSTM_HEADS + h:MLSTM_HEADS + h + 1]
    lf_col = _log_sigmoid(f_col)
    lf_row = _col_to_row(lf_col, eye)
    i_row = _col_to_row(i_col, eye)
    fcum_col = jnp.sum(jnp.where(causal, lf_row, 0.0), axis=1, keepdims=True)
    fcum_row = _col_to_row(fcum_col, eye)
    ftot_col = jnp.sum(jnp.where(same, lf_row, 0.0), axis=1, keepdims=True)
    return i_col, i_row, fcum_col, fcum_row, ftot_col


def _mlstm_prompt_kernel(q_ref, k_ref, v0_ref, v1_ref, o0_ref, o1_ref, zg_ref, bias_ref, gm_ref,
                         yb_ref, c_ref, n_ref, m_ref, c_scr, n_scr, m_scr):
    L = q_ref.shape[0]
    dk = q_ref.shape[1] // MLSTM_HEADS
    dv = 2 * v0_ref.shape[1] // MLSTM_HEADS
    step = pl.program_id(1)

    @pl.when(step == 0)
    def _():
        c_scr[...] = jnp.zeros_like(c_scr)
        n_scr[...] = jnp.zeros_like(n_scr)
        m_scr[...] = jnp.zeros_like(m_scr)

    ri = lax.broadcasted_iota(jnp.int32, (L, L), 0)
    ci = lax.broadcasted_iota(jnp.int32, (L, L), 1)
    causal = ci <= ri
    eye = ci == ri
    same = ci >= 0
    for h in range(MLSTM_HEADS):
        q = q_ref[:, h * dk:(h + 1) * dk] * (dk ** -0.5)
        k = k_ref[:, h * dk:(h + 1) * dk]
        v = _head_blocks((v0_ref, v1_ref), h, dv)
        o = _head_blocks((o0_ref, o1_ref), h, dv)
        i_col, i_row, fcum_col, fcum_row, ftot = _mlstm_gates(zg_ref, bias_ref, h, same, causal, eye)
        m_prev = m_scr[h:h + 1, 0:1]
        dlog = jnp.where(causal, fcum_col - fcum_row + i_row, -jnp.inf)
        m_inter = m_prev + fcum_col
        m_tok = jnp.maximum(m_inter, jnp.max(dlog, axis=1, keepdims=True))
        qb = q.astype(BF16)
        kb = k.astype(BF16)
        vb = v.astype(BF16)
        s = lax.dot_general(qb, kb, (((1,), (1,)), ((), ())), preferred_element_type=F32) * jnp.exp(dlog - m_tok)
        w_inter = jnp.exp(m_inter - m_tok)
        cmat = c_scr[h]
        nvec = n_scr[h:h + 1, :]
        num = _dot(s.astype(BF16), vb) + w_inter * _dot(qb, cmat.astype(BF16))
        den = jnp.sum(s, axis=1, keepdims=True) + w_inter * jnp.sum(q * nvec, axis=1, keepdims=True)
        den = jnp.maximum(jnp.abs(den), jnp.exp(-m_tok))
        hh = num / den
        ms = jnp.mean(hh * hh, axis=-1, keepdims=True)
        yn = (hh * lax.rsqrt(ms + EPS)) * gm_ref[:, h * dv:(h + 1) * dv]
        yb_ref[:, h * dv:(h + 1) * dv] = (yn * jax.nn.sigmoid(o)).astype(yb_ref.dtype)
        wlog = ftot - fcum_col + i_col
        m_new = jnp.maximum(m_prev + ftot[0:1, :], jnp.max(wlog, axis=0, keepdims=True))
        wk = jnp.exp(wlog - m_new)
        decay = jnp.exp(m_prev + ftot[0:1, :] - m_new)
        kw = k * wk
        c_scr[h] = decay * cmat + lax.dot_general(kw.astype(BF16), vb, (((0,), (0,)), ((), ())),
                                                  preferred_element_type=F32)
        n_scr[h:h + 1, :] = decay * nvec + jnp.sum(kw, axis=0, keepdims=True)
        m_scr[h:h + 1, :] = jnp.broadcast_to(m_new, (1, m_scr.shape[1]))

    @pl.when(step == pl.num_programs(1) - 1)
    def _():
        c_ref[0] = c_scr[...]
        n_ref[0] = n_scr[...]
        m_ref[0] = m_scr[0:MLSTM_HEADS, 0:1]


def _mlstm_prompt(z, zg, bias, g_m, batch, seq, q_blk):
    m = z.shape[0]
    wb = g_m.shape[1]
    qw = wb // 2
    dk = qw // MLSTM_HEADS
    dv = wb // MLSTM_HEADS
    L = _pick(seq, (256, 128, 64, 32, 16, 8))
    nc = seq // L
    row = lambda b, c: b * nc + c
    zspec = lambda blk: pl.BlockSpec((L, qw), lambda b, c: (row(b, c), blk))
    est = 2 * 6 * L * qw * 4 + 2 * L * GATE_LANES * 4 + 2 * L * wb * 2 + 3 * MLSTM_HEADS * dk * dv * 4 + 16 * L * L * 4
    return pl.pallas_call(
        _mlstm_prompt_kernel,
        grid=(batch, nc),
        in_specs=[zspec(q_blk + i) for i in range(6)] + [
            pl.BlockSpec((L, GATE_LANES), lambda b, c: (row(b, c), 0)),
            pl.BlockSpec((1, GATE_LANES), lambda b, c: (0, 0)),
            pl.BlockSpec((1, wb), lambda b, c: (0, 0)),
        ],
        out_specs=[
            pl.BlockSpec((L, wb), lambda b, c: (row(b, c), 0)),
            pl.BlockSpec((1, MLSTM_HEADS, dk, dv), lambda b, c: (b, 0, 0, 0)),
            pl.BlockSpec((1, MLSTM_HEADS, dk), lambda b, c: (b, 0, 0)),
            pl.BlockSpec((1, MLSTM_HEADS, 1), lambda b, c: (b, 0, 0)),
        ],
        out_shape=[
            jax.ShapeDtypeStruct((m, wb), BF16),
            jax.ShapeDtypeStruct((batch, MLSTM_HEADS, dk, dv), F32),
            jax.ShapeDtypeStruct((batch, MLSTM_HEADS, dk), F32),
            jax.ShapeDtypeStruct((batch, MLSTM_HEADS, 1), F32),
        ],
        scratch_shapes=[
            pltpu.VMEM((MLSTM_HEADS, dk, dv), F32),
            pltpu.VMEM((MLSTM_HEADS, dk), F32),
            pltpu.VMEM((8, 128), F32),
        ],
        compiler_params=_params(("parallel", "arbitrary"), est),
    )(z, z, z, z, z, z, zg, bias, g_m)


def _mlstm_sample_kernel(q_ref, k_ref, v0_ref, v1_ref, o0_ref, o1_ref, zg_ref, bias_ref, gm_ref,
                         c0_ref, nrow_ref, mrow_ref, n0_ref,
                         yb_ref, c_ref, n_ref, m_ref, *, seq_rows):
    R = q_ref.shape[0]
    nb = R // seq_rows
    dk = q_ref.shape[1] // MLSTM_HEADS
    dv = 2 * v0_ref.shape[1] // MLSTM_HEADS
    ri = lax.broadcasted_iota(jnp.int32, (R, R), 0)
    ci = lax.broadcasted_iota(jnp.int32, (R, R), 1)
    same = _seq_of(ri, seq_rows) == _seq_of(ci, seq_rows)
    causal = same & (ci <= ri)
    eye = ci == ri
    rid = lax.broadcasted_iota(jnp.int32, (R, 1), 0)
    for h in range(MLSTM_HEADS):
        q = q_ref[:, h * dk:(h + 1) * dk] * (dk ** -0.5)
        k = k_ref[:, h * dk:(h + 1) * dk]
        v = _head_blocks((v0_ref, v1_ref), h, dv)
        o = _head_blocks((o0_ref, o1_ref), h, dv)
        i_col, i_row, fcum_col, fcum_row, ftot = _mlstm_gates(zg_ref, bias_ref, h, same, causal, eye)
        m_prev = mrow_ref[:, h:h + 1]
        dlog = jnp.where(causal, fcum_col - fcum_row + i_row, -jnp.inf)
        m_inter = m_prev + fcum_col
        m_tok = jnp.maximum(m_inter, jnp.max(dlog, axis=1, keepdims=True))
        qb = q.astype(BF16)
        kb = k.astype(BF16)
        vb = v.astype(BF16)
        s = lax.dot_general(qb, kb, (((1,), (1,)), ((), ())), preferred_element_type=F32) * jnp.exp(dlog - m_tok)
        w_inter = jnp.exp(m_inter - m_tok)
        inter = jnp.zeros((R, dv), F32)
        for b in range(nb):
            mine = _seq_of(rid, seq_rows) == b
            inter = inter + jnp.where(mine, _dot(qb, c0_ref[b, h].astype(BF16)), 0.0)
        num = _dot(s.astype(BF16), vb) + w_inter * inter
        nrow = nrow_ref[:, h * dk:(h + 1) * dk]
        den = jnp.sum(s, axis=1, keepdims=True) + w_inter * jnp.sum(q * nrow, axis=1, keepdims=True)
        den = jnp.maximum(jnp.abs(den), jnp.exp(-m_tok))
        hh = num / den
        ms = jnp.mean(hh * hh, axis=-1, keepdims=True)
        yn = (hh * lax.rsqrt(ms + EPS)) * gm_ref[:, h * dv:(h + 1) * dv]
        yb_ref[:, h * dv:(h + 1) * dv] = (yn * jax.nn.sigmoid(o)).astype(yb_ref.dtype)
        wlog_col = ftot - fcum_col + i_col
        wlog_row = _col_to_row(wlog_col, eye)
        m_new = jnp.maximum(m_prev + ftot, jnp.max(jnp.where(same, wlog_row, -jnp.inf), axis=1, keepdims=True))
        wk = jnp.exp(wlog_col - m_new)
        decay = jnp.exp(m_prev + ftot - m_new)
        kw = k * wk
        for b in range(nb):
            mine = _seq_of(rid, seq_rows) == b
            kw_b = jnp.where(mine, kw, 0.0)
            r0 = b * seq_rows
            dec_b = decay[r0:r0 + 1, :]
            upd = lax.dot_general(kw_b.astype(BF16), vb, (((0,), (0,)), ((), ())), preferred_element_type=F32)
            c_ref[b, h] = dec_b * c0_ref[b, h] + upd
            n_ref[b, h:h + 1, :] = dec_b * n0_ref[b, h:h + 1, :] + jnp.sum(kw_b, axis=0, keepdims=True)
            m_ref[0, b:b + 1, h:h + 1] = m_new[r0:r0 + 1, :]


def _mlstm_sample(z, zg, bias, g_m, c0, n0, m0, seq_rows, q_blk):
    m = z.shape[0]
    batch = c0.shape[0]
    wb = g_m.shape[1]
    qw = wb // 2
    dk = qw // MLSTM_HEADS
    dv = wb // MLSTM_HEADS
    nb = 4
    R = nb * seq_rows
    n_rows = jnp.repeat(n0.reshape(batch, MLSTM_HEADS * dk), seq_rows, axis=0)
    m_rows = jnp.repeat(m0, seq_rows, axis=0)
    zspec = lambda blk: pl.BlockSpec((R, qw), lambda s: (s, blk))
    cblk = nb * MLSTM_HEADS * dk * dv * 4
    est = 4 * cblk + 2 * 7 * R * qw * 4 + 2 * R * GATE_LANES * 4 + 2 * R * wb * 2
    yb, c1, n1, m1 = pl.pallas_call(
        functools.partial(_mlstm_sample_kernel, seq_rows=seq_rows),
        grid=(m // R,),
        in_specs=[zspec(q_blk + i) for i in range(6)] + [
            pl.BlockSpec((R, GATE_LANES), lambda s: (s, 0)),
            pl.BlockSpec((1, GATE_LANES), lambda s: (0, 0)),
            pl.BlockSpec((1, wb), lambda s: (0, 0)),
            pl.BlockSpec((nb, MLSTM_HEADS, dk, dv), lambda s: (s, 0, 0, 0)),
            pl.BlockSpec((R, qw), lambda s: (s, 0)),
            pl.BlockSpec((R, MLSTM_HEADS), lambda s: (s, 0)),
            pl.BlockSpec((nb, MLSTM_HEADS, dk), lambda s: (s, 0, 0)),
        ],
        out_specs=[
            pl.BlockSpec((R, wb), lambda s: (s, 0)),
            pl.BlockSpec((nb, MLSTM_HEADS, dk, dv), lambda s: (s, 0, 0, 0)),
            pl.BlockSpec((nb, MLSTM_HEADS, dk), lambda s: (s, 0, 0)),
            pl.BlockSpec((1, nb, MLSTM_HEADS), lambda s: (s, 0, 0)),
        ],
        out_shape=[
            jax.ShapeDtypeStruct((m, wb), BF16),
            jax.ShapeDtypeStruct(c0.shape, F32),
            jax.ShapeDtypeStruct(n0.shape, F32),
            jax.ShapeDtypeStruct((batch // nb, nb, MLSTM_HEADS), F32),
        ],
        compiler_params=_params(("parallel",), est),
    )(z, z, z, z, z, z, zg, bias, g_m, c0, n_rows, m_rows, n0)
    return yb, c1, n1, m1.reshape(batch, MLSTM_HEADS)


def _tile_interleave(a, b, tb):
    lead = a.shape[:-1]
    n = a.shape[-1]
    st = jnp.stack([a.reshape(lead + (n // tb, tb)), b.reshape(lead + (n // tb, tb))], axis=-2)
    return st.reshape(lead + (2 * n,))


def kernel(x_prompt, x_sample, state_pool, state_mlstm_c, state_mlstm_n, state_mlstm_m, p_prompt, p_sample, g_mix, w_in, b_igate, b_fgate, w_pool, pool_scale, g_mlstm, w_s, b_s, g_v, w_out, g_ffn, w_ffn_gate, w_ffn_up, w_ffn_down, g_ple, w_ple_gate, w_ple_proj, g_final):
    batch, seq, d = x_prompt.shape
    dec_batch, dec_seq, _ = x_sample.shape
    depth = w_in.shape[0]
    wa = pool_scale.shape[1]
    wb = g_mlstm.shape[1]
    wc = g_v.shape[1]
    d_ff = w_ffn_gate.shape[2]
    off_gate = wa + 2 * (wb // 2) + 2 * wb
    off_c = off_gate + 2 * MLSTM_HEADS
    q_blk = wa // (wb // 2)
    u_blk = off_gate // wc
    tb = _pick(d_ff, (256, 128))

    w_main = jnp.concatenate([w_in[:, :, :off_gate], w_in[:, :, off_c:]], axis=2).astype(BF16)
    w_gate = jnp.pad(w_in[:, :, off_gate:off_c], ((0, 0), (0, 0), (0, GATE_LANES - 2 * MLSTM_HEADS))).astype(BF16)
    gate_bias = jnp.pad(jnp.concatenate([b_igate, b_fgate], axis=1),
                        ((0, 0), (0, GATE_LANES - 2 * MLSTM_HEADS)))[:, None, :]
    w_pool_b = w_pool.astype(BF16)
    w_out_b = w_out.astype(BF16)
    w_gu = _tile_interleave(w_ffn_gate.astype(BF16), w_ffn_up.astype(BF16), tb)
    w_fd = w_ffn_down.astype(BF16)
    w_pg = w_ple_gate.astype(BF16)
    w_pp = w_ple_proj.astype(BF16)
    reps = CHUNK // dec_seq
    ws_sample = jnp.tile(w_s[:, :, :dec_seq, :dec_seq], (1, 1, reps, reps))
    bias_prompt = jnp.swapaxes(b_s, 1, 2)
    bias_sample = jnp.tile(jnp.swapaxes(b_s[:, :, :dec_seq], 1, 2), (1, reps, 1))

    xp = x_prompt.reshape(batch * seq, d)
    xs = x_sample.reshape(dec_batch * dec_seq, d)
    v_start = ((seq - 1) // CHUNK) * CHUNK
    outs = {k: [] for k in ("pool_p", "pool_s", "c_p", "c_s", "n_p", "n_s", "m_p", "m_s", "v_p", "v_s")}

    for i in range(depth):
        row = lambda a: a[i][None, :]
        zp, zgp = _inproj(xp, row(g_mix), w_main[i], w_gate[i])
        ya = _pool_prompt(zp, seq, w_pool_b[i], row(pool_scale))
        yb, c1, n1, m1 = _mlstm_prompt(zp, zgp, gate_bias[i], row(g_mlstm), batch, seq, q_blk)
        yc, vn = _chunk_mlp(zp, u_blk, w_s[i], bias_prompt[i], row(g_v), CHUNK)
        xp = _wout(ya, yb, yc, w_out_b[i], xp)
        xp = _ffn_down(_ffn_up(xp, row(g_ffn), w_gu[i], tb), w_fd[i], xp)
        xp = _ple(xp, row(g_ple), w_pg[i], p_prompt[i].reshape(batch * seq, -1), w_pp[i])
        outs["pool_p"].append(zp.reshape(batch, seq, -1)[:, seq - POOL_BUF:, :wa])
        outs["c_p"].append(c1)
        outs["n_p"].append(n1)
        outs["m_p"].append(m1.reshape(batch, MLSTM_HEADS))
        outs["v_p"].append(vn.reshape(batch, seq, wc)[:, v_start:])
        zs, zgs = _inproj(xs, row(g_mix), w_main[i], w_gate[i])
        a_ext = jnp.concatenate([state_pool[i], zs[:, :wa].reshape(dec_batch, dec_seq, wa)], axis=1)
        ya = _pool_sample(a_ext, w_pool_b[i], row(pool_scale), PAST_LEN).reshape(dec_batch * dec_seq, wa)
        yb, c1, n1, m1 = _mlstm_sample(zs, zgs, gate_bias[i], row(g_mlstm), state_mlstm_c[i], state_mlstm_n[i],
                                       state_mlstm_m[i], dec_seq, q_blk)
        yc, vn = _chunk_mlp(zs, u_blk, ws_sample[i], bias_sample[i], row(g_v), dec_seq)
        xs = _wout(ya, yb, yc, w_out_b[i], xs)
        xs = _ffn_down(_ffn_up(xs, row(g_ffn), w_gu[i], tb), w_fd[i], xs)
        xs = _ple(xs, row(g_ple), w_pg[i], p_sample[i].reshape(dec_batch * dec_seq, -1), w_pp[i])
        outs["pool_s"].append(a_ext[:, -POOL_BUF:])
        outs["c_s"].append(c1)
        outs["n_s"].append(n1)
        outs["m_s"].append(m1)
        outs["v_s"].append(vn.reshape(dec_batch, dec_seq, wc))

    y_prompt = _final_norm(xp, g_final[None, :]).reshape(batch, seq, d)
    y_sample = _final_norm(xs, g_final[None, :]).reshape(dec_batch, dec_seq, d)
    st = lambda k: jnp.stack(outs[k])
    return (y_prompt, y_sample, st("pool_p"), st("pool_s"), st("c_p"), st("c_s"), st("n_p"), st("n_s"),
            st("m_p"), st("m_s"), st("v_p"), st("v_s"))
```

```python
import functools
import math

import jax
import jax.numpy as jnp
from jax import lax
from jax.experimental import pallas as pl
from jax.experimental.pallas import tpu as pltpu

F32 = jnp.float32
BF16 = jnp.bfloat16

EPS = 1e-6
PAST_LEN = 16384
POOL_WINDOWS = (2, 4, 8, 16)
POOL_BUF = max(POOL_WINDOWS) - 1
POOL_HALO = 16
N_POOL = len(POOL_WINDOWS)
MLSTM_HEADS = 4
N_CM = 4
CHUNK = 128
GATE_LANES = 128
SS_LANES = 128

VMEM_SLACK_BYTES = 10 * 2**20
VMEM_CAP_BYTES = 58 * 2**20

_ROW_TILES = (1024, 512, 256, 128, 64, 32, 16, 8)


def _params(sem, vmem_estimate):
    limit = min(int(vmem_estimate) + VMEM_SLACK_BYTES, VMEM_CAP_BYTES)
    return pltpu.CompilerParams(dimension_semantics=sem, vmem_limit_bytes=limit)


def _pick(n, candidates):
    for c in candidates:
        if n % c == 0:
            return c
    return n


def _single(shape, index_map):
    return pl.BlockSpec(shape, index_map, pipeline_mode=pl.Buffered(1))


def _layer_spec(arr, layer):
    nd = arr.ndim - 1
    return pl.BlockSpec((None,) + arr.shape[1:], lambda *_: (layer,) + (0,) * nd)


def _dot(a, b):
    return jnp.dot(a, b, preferred_element_type=F32)


def _dot_nt(a, bt):
    return lax.dot_general(a, bt, (((1,), (1,)), ((), ())), preferred_element_type=F32)


def _gelu_tanh(x):
    return 0.5 * x * (1.0 + jnp.tanh(math.sqrt(2.0 / math.pi) * (x + 0.044715 * (x * x * x))))


def _seq_of(idx, seq_rows):
    if seq_rows & (seq_rows - 1) == 0:
        return lax.shift_right_logical(idx, seq_rows.bit_length() - 1)
    return idx // seq_rows


def _log_sigmoid(x):
    return jnp.minimum(x, 0.0) - jnp.log(1.0 + jnp.exp(-jnp.abs(x)))


def _row_scale(ss_ref, width):
    return lax.rsqrt(ss_ref[:, 0:1] * (1.0 / width) + EPS)


def _emit_rows(xn, j, o_ref, xb_ref, ss_ref):
    o_ref[...] = xn
    xb_ref[...] = xn.astype(xb_ref.dtype)
    part = jnp.broadcast_to(jnp.sum(xn * xn, axis=-1, keepdims=True), ss_ref.shape)

    @pl.when(j == 0)
    def _():
        ss_ref[...] = part

    @pl.when(j > 0)
    def _():
        ss_ref[...] += part


def _row_outs(m, d, bm, bn):
    specs = [
        pl.BlockSpec((bm, bn), lambda i, j: (i, j)),
        pl.BlockSpec((bm, bn), lambda i, j: (i, j)),
        pl.BlockSpec((bm, SS_LANES), lambda i, j: (i, 0)),
    ]
    shapes = [jax.ShapeDtypeStruct((m, d), F32), jax.ShapeDtypeStruct((m, d), BF16),
              jax.ShapeDtypeStruct((m, SS_LANES), F32)]
    return specs, shapes


def _prep_rows_kernel(x_ref, xb_ref, ss_ref):
    x = x_ref[...]
    xb_ref[...] = x.astype(xb_ref.dtype)
    ss_ref[...] = jnp.broadcast_to(jnp.sum(x * x, axis=-1, keepdims=True), ss_ref.shape)


def _prep_rows(x):
    m, d = x.shape
    bm = _pick(m, (256, 128, 64, 32, 16, 8))
    return pl.pallas_call(
        _prep_rows_kernel,
        grid=(m // bm,),
        in_specs=[pl.BlockSpec((bm, d), lambda i: (i, 0))],
        out_specs=[pl.BlockSpec((bm, d), lambda i: (i, 0)), pl.BlockSpec((bm, SS_LANES), lambda i: (i, 0))],
        out_shape=[jax.ShapeDtypeStruct((m, d), BF16), jax.ShapeDtypeStruct((m, SS_LANES), F32)],
        compiler_params=_params(("parallel",), 2 * bm * d * 6),
    )(x)


def _inproj_kernel(xb_ref, ss_ref, wt_ref, wgt_ref, z_ref, zg_ref):
    rs = _row_scale(ss_ref, xb_ref.shape[1])

    @pl.when(pl.program_id(1) == 0)
    def _():
        zg_ref[...] = rs * _dot_nt(xb_ref[...], wgt_ref[...])

    z_ref[...] = rs * _dot_nt(xb_ref[...], wt_ref[...])


def _inproj(xb, ss, wt_main, wt_gate, layer):
    m, d = xb.shape
    n = wt_main.shape[1]
    bm = _pick(m, _ROW_TILES)
    bn = _pick(n, (1024, 512, 256, 128))
    est = 2 * bm * d * 2 + 2 * d * bn * 2 + 2 * d * GATE_LANES * 2 + 2 * bm * bn * 4 + 4 * bm * GATE_LANES * 4
    return pl.pallas_call(
        _inproj_kernel,
        grid=(m // bm, n // bn),
        in_specs=[
            pl.BlockSpec((bm, d), lambda i, j: (i, 0)),
            pl.BlockSpec((bm, SS_LANES), lambda i, j: (i, 0)),
            pl.BlockSpec((None, bn, d), lambda i, j: (layer, j, 0)),
            _layer_spec(wt_gate, layer),
        ],
        out_specs=[
            pl.BlockSpec((bm, bn), lambda i, j: (i, j)),
            pl.BlockSpec((bm, GATE_LANES), lambda i, j: (i, 0)),
        ],
        out_shape=[jax.ShapeDtypeStruct((m, n), F32), jax.ShapeDtypeStruct((m, GATE_LANES), F32)],
        compiler_params=_params(("parallel", "arbitrary"), est),
    )(xb, ss, wt_main, wt_gate)


def _ffn_up_kernel(xb_ref, ss_ref, wg_ref, wu_ref, o_ref):
    rs = _row_scale(ss_ref, xb_ref.shape[1])
    h = xb_ref[...]
    a = rs * _dot(h, wg_ref[...])
    b = rs * _dot(h, wu_ref[...])
    o_ref[...] = ((a * jax.nn.sigmoid(a)) * b).astype(o_ref.dtype)


def _ffn_up(xb, ss, w_gate, w_up, layer):
    m, d = xb.shape
    n = w_gate.shape[2]
    bm = _pick(m, _ROW_TILES)
    tb = _pick(n, (256, 128))
    est = 2 * bm * d * 2 + 4 * d * tb * 2 + 2 * bm * tb * 2 + 2 * bm * SS_LANES * 4
    wspec = pl.BlockSpec((None, d, tb), lambda i, j: (layer, 0, j))
    return pl.pallas_call(
        _ffn_up_kernel,
        grid=(m // bm, n // tb),
        in_specs=[pl.BlockSpec((bm, d), lambda i, j: (i, 0)), pl.BlockSpec((bm, SS_LANES), lambda i, j: (i, 0)),
                  wspec, wspec],
        out_specs=pl.BlockSpec((bm, tb), lambda i, j: (i, j)),
        out_shape=jax.ShapeDtypeStruct((m, n), BF16),
        compiler_params=_params(("parallel", "arbitrary"), est),
    )(xb, ss, w_gate, w_up)


def _ple_kernel(xb_ref, ss_ref, w_ref, p_ref, wpp_ref, x_ref, o_ref, xbo_ref, sso_ref):
    rs = _row_scale(ss_ref, xb_ref.shape[1])
    gate = jax.nn.sigmoid(rs * _dot(xb_ref[...], w_ref[...]))
    emb = _dot(p_ref[...].astype(BF16), wpp_ref[...])
    _emit_rows(x_ref[...] + gate * emb, pl.program_id(1), o_ref, xbo_ref, sso_ref)


def _ple(xb, ss, w_pg, p, w_pp, x, layer):
    m, d = x.shape
    pd = p.shape[2]
    bm = _pick(m, _ROW_TILES)
    bn = _pick(d, (512, 256, 128))
    est = (2 * bm * d * 2 + 2 * d * bn * 2 + 2 * bm * pd * 4 + 2 * pd * bn * 2 + 4 * bm * bn * 4
           + 2 * bm * bn * 2 + 4 * bm * SS_LANES * 4)
    out_specs, out_shape = _row_outs(m, d, bm, bn)
    return pl.pallas_call(
        _ple_kernel,
        grid=(m // bm, d // bn),
        in_specs=[
            pl.BlockSpec((bm, d), lambda i, j: (i, 0)),
            pl.BlockSpec((bm, SS_LANES), lambda i, j: (i, 0)),
            pl.BlockSpec((None, d, bn), lambda i, j: (layer, 0, j)),
            pl.BlockSpec((None, bm, pd), lambda i, j: (layer, i, 0)),
            pl.BlockSpec((None, pd, bn), lambda i, j: (layer, 0, j)),
            pl.BlockSpec((bm, bn), lambda i, j: (i, j)),
        ],
        out_specs=out_specs,
        out_shape=out_shape,
        compiler_params=_params(("parallel", "arbitrary"), est),
    )(xb, ss, w_pg, p, w_pp, x)


def _wout_kernel(ya_ref, yb_ref, yc_ref, w_ref, x_ref, o_ref, xbo_ref, sso_ref, a_scr):
    j = pl.program_id(1)

    @pl.when(j == 0)
    def _():
        wa = ya_ref.shape[1]
        wb = yb_ref.shape[1]
        a_scr[:, :wa] = ya_ref[...].astype(a_scr.dtype)
        a_scr[:, wa:wa + wb] = yb_ref[...].astype(a_scr.dtype)
        a_scr[:, wa + wb:] = yc_ref[...].astype(a_scr.dtype)

    _emit_rows(x_ref[...] + _dot(a_scr[...], w_ref[...]), j, o_ref, xbo_ref, sso_ref)


def _wout(ya, yb, yc, w, x, layer):
    m, d = x.shape
    wa, wb, wc = ya.shape[1], yb.shape[1], yc.shape[1]
    k = wa + wb + wc
    bm = _pick(m, _ROW_TILES)
    bn = _pick(d, (512, 256, 128))
    est = (bm * wa * ya.dtype.itemsize + bm * wb * 2 + bm * wc * 2 + bm * k * 2
           + 2 * k * bn * 2 + 4 * bm * bn * 4 + 2 * bm * bn * 2 + 2 * bm * SS_LANES * 4)
    out_specs, out_shape = _row_outs(m, d, bm, bn)
    return pl.pallas_call(
        _wout_kernel,
        grid=(m // bm, d // bn),
        in_specs=[
            _single((bm, wa), lambda i, j: (i, 0)),
            _single((bm, wb), lambda i, j: (i, 0)),
            _single((bm, wc), lambda i, j: (i, 0)),
            pl.BlockSpec((None, k, bn), lambda i, j: (layer, 0, j)),
            pl.BlockSpec((bm, bn), lambda i, j: (i, j)),
        ],
        out_specs=out_specs,
        out_shape=out_shape,
        scratch_shapes=[pltpu.VMEM((bm, k), BF16)],
        compiler_params=_params(("parallel", "arbitrary"), est),
    )(ya, yb, yc, w, x)


def _ffn_down_kernel(a_ref, w_ref, x_ref, o_ref, xbo_ref, sso_ref):
    _emit_rows(x_ref[...] + _dot(a_ref[...], w_ref[...]), pl.program_id(1), o_ref, xbo_ref, sso_ref)


def _ffn_down(a, w, x, layer):
    m, d = x.shape
    k = a.shape[1]
    bm = _pick(m, _ROW_TILES)
    bn = _pick(d, (256, 128))
    est = bm * k * 2 + 2 * k * bn * 2 + 4 * bm * bn * 4 + 2 * bm * bn * 2 + 2 * bm * SS_LANES * 4
    out_specs, out_shape = _row_outs(m, d, bm, bn)
    return pl.pallas_call(
        _ffn_down_kernel,
        grid=(m // bm, d // bn),
        in_specs=[
            _single((bm, k), lambda i, j: (i, 0)),
            pl.BlockSpec((None, k, bn), lambda i, j: (layer, 0, j)),
            pl.BlockSpec((bm, bn), lambda i, j: (i, j)),
        ],
        out_specs=out_specs,
        out_shape=out_shape,
        compiler_params=_params(("parallel", "arbitrary"), est),
    )(a, w, x)


def _final_norm_kernel(x_ref, g_ref, o_ref):
    x = x_ref[...]
    ms = jnp.mean(x * x, axis=-1, keepdims=True)
    o_ref[...] = (x * lax.rsqrt(ms + EPS)) * g_ref[...]


def _final_norm(x, g):
    m, d = x.shape
    bm = _pick(m, (256, 128, 64, 32, 16, 8))
    return pl.pallas_call(
        _final_norm_kernel,
        grid=(m // bm,),
        in_specs=[pl.BlockSpec((bm, d), lambda i: (i, 0)), pl.BlockSpec((1, d), lambda i: (0, 0))],
        out_specs=pl.BlockSpec((bm, d), lambda i: (i, 0)),
        out_shape=jax.ShapeDtypeStruct((m, d), F32),
        compiler_params=_params(("parallel",), 4 * bm * d * 4),
    )(x, g)


def _pool_prompt_kernel(a_ref, halo_ref, wp_ref, sc_ref, y_ref, ext_scr, *, tiles_per_seq):
    tp = a_ref.shape[0]
    cg = wp_ref.shape[1]
    tile = pl.program_id(0) % tiles_per_seq
    ext_scr[0:POOL_HALO, :] = jnp.where(tile == 0, 0.0, halo_ref[...])
    ext_scr[POOL_HALO:, :] = a_ref[...]
    pos = tile * tp + lax.broadcasted_iota(jnp.int32, (tp, 1), 0)
    for g, w in enumerate(POOL_WINDOWS):
        sl = slice(g * cg, (g + 1) * cg)
        s = ext_scr[POOL_HALO:POOL_HALO + tp, sl]
        for k in range(1, w):
            s = s + ext_scr[POOL_HALO - k:POOL_HALO - k + tp, sl]
        cnt = jnp.minimum(pos + 1, w).astype(F32)
        dlt = s / cnt - a_ref[:, sl]
        y = _dot(dlt.astype(BF16), wp_ref[g]) * sc_ref[:, sl]
        y_ref[:, sl] = y.astype(y_ref.dtype)


def _pool_prompt(z, seq, w_pool, scale, layer):
    m = z.shape[0]
    wa = scale.shape[2]
    tp = _pick(seq, (512, 256, 128, 64, 32, 16))
    tiles_per_seq = seq // tp
    halo_blocks = tp // POOL_HALO
    est = (2 * tp * wa * 4 + 2 * POOL_HALO * wa * 4 + (tp + POOL_HALO) * wa * 4 + 2 * tp * wa * 2
           + 4 * wa * wa // N_POOL)
    return pl.pallas_call(
        functools.partial(_pool_prompt_kernel, tiles_per_seq=tiles_per_seq),
        grid=(m // tp,),
        in_specs=[
            pl.BlockSpec((tp, wa), lambda i: (i, 0)),
            pl.BlockSpec((POOL_HALO, wa), lambda i: (jnp.maximum(i * halo_blocks - 1, 0), 0)),
            _layer_spec(w_pool, layer),
            _layer_spec(scale, layer),
        ],
        out_specs=pl.BlockSpec((tp, wa), lambda i: (i, 0)),
        out_shape=jax.ShapeDtypeStruct((m, wa), BF16),
        scratch_shapes=[pltpu.VMEM((tp + POOL_HALO, wa), F32)],
        compiler_params=_params(("parallel",), est),
    )(z, z, w_pool, scale)


def _pool_sample_kernel(hist_ref, new_ref, wp_ref, sc_ref, y_ref, *, first_pos):
    cg = wp_ref.shape[1]
    n_new = new_ref.shape[0]
    rows = [hist_ref[r] for r in range(POOL_BUF)] + [new_ref[t] for t in range(n_new)]
    for t in range(n_new):
        for g, w in enumerate(POOL_WINDOWS):
            sl = slice(g * cg, (g + 1) * cg)
            s = rows[POOL_BUF + t][:, sl]
            for k in range(1, w):
                s = s + rows[POOL_BUF + t - k][:, sl]
            cnt = float(min(first_pos + t + 1, w))
            dlt = s / cnt - rows[POOL_BUF + t][:, sl]
            y = _dot(dlt.astype(BF16), wp_ref[g]) * sc_ref[:, sl]
            y_ref[t, :, sl] = y.astype(y_ref.dtype)


def _pool_sample(hist, new, w_pool, scale, first_pos, layer):
    n_new, b, wa = new.shape
    bb = _pick(b, (32, 16, 8))
    est = 2 * (POOL_BUF + 2 * n_new) * bb * wa * 4 + 4 * wa * wa // N_POOL
    return pl.pallas_call(
        functools.partial(_pool_sample_kernel, first_pos=first_pos),
        grid=(b // bb,),
        in_specs=[
            pl.BlockSpec((None, POOL_BUF, bb, wa), lambda i: (layer, 0, i, 0)),
            pl.BlockSpec((n_new, bb, wa), lambda i: (0, i, 0)),
            _layer_spec(w_pool, layer),
            _layer_spec(scale, layer),
        ],
        out_specs=pl.BlockSpec((n_new, bb, wa), lambda i: (0, i, 0)),
        out_shape=jax.ShapeDtypeStruct((n_new, b, wa), F32),
        compiler_params=_params(("parallel",), est),
    )(hist, new, w_pool, scale)


def _chunk_mlp_kernel(u_ref, v_ref, ws_ref, b_ref, gv_ref, yc_ref, vn_ref, *, seq_rows):
    rows = u_ref.shape[0]
    cg = u_ref.shape[1] // N_CM
    ri = lax.broadcasted_iota(jnp.int32, (rows, rows), 0)
    ci = lax.broadcasted_iota(jnp.int32, (rows, rows), 1)
    mask = ci <= ri
    if seq_rows < rows:
        mask = mask & (_seq_of(ri, seq_rows) == _seq_of(ci, seq_rows))
    for g in range(N_CM):
        sl = slice(g * cg, (g + 1) * cg)
        u = _gelu_tanh(u_ref[:, sl])
        v = _gelu_tanh(v_ref[:, sl])
        ms = jnp.mean(v * v, axis=-1, keepdims=True)
        vn = (v * lax.rsqrt(ms + EPS)) * gv_ref[:, sl]
        vn_ref[:, sl] = vn
        wm = jnp.where(mask, ws_ref[g], 0.0).astype(BF16)
        mix = _dot(wm, vn.astype(BF16)) + b_ref[:, g:g + 1]
        yc_ref[:, sl] = (u * mix).astype(yc_ref.dtype)


def _chunk_mlp(z, u_blk, w_mix, bias, g_v, seq_rows, layer):
    m = z.shape[0]
    wc = g_v.shape[2]
    est = 4 * CHUNK * wc * 4 + 2 * N_CM * CHUNK * CHUNK * 4 + 2 * CHUNK * wc * 2 + 2 * CHUNK * wc * 4
    return pl.pallas_call(
        functools.partial(_chunk_mlp_kernel, seq_rows=seq_rows),
        grid=(m // CHUNK,),
        in_specs=[
            pl.BlockSpec((CHUNK, wc), lambda i: (i, u_blk)),
            pl.BlockSpec((CHUNK, wc), lambda i: (i, u_blk + 1)),
            _layer_spec(w_mix, layer),
            _layer_spec(bias, layer),
            _layer_spec(g_v, layer),
        ],
        out_specs=[pl.BlockSpec((CHUNK, wc), lambda i: (i, 0)), pl.BlockSpec((CHUNK, wc), lambda i: (i, 0))],
        out_shape=[jax.ShapeDtypeStruct((m, wc), BF16), jax.ShapeDtypeStruct((m, wc), F32)],
        compiler_params=_params(("parallel",), est),
    )(z, z, w_mix, bias, g_v)


def _col_to_row(col, eye):
    return jnp.sum(jnp.where(eye, col, 0.0), axis=0, keepdims=True)


def _head_blocks(refs, h, width):
    r = refs[h // 2]
    return r[:, (h % 2) * width:(h % 2 + 1) * width]


def _mlstm_gates(zg_ref, bias_ref, h, same, causal, eye):
    i_col = zg_ref[:, h:h + 1] + bias_ref[:, h:h + 1]
    f_col = zg_ref[:, MLSTM_HEADS + h:MLSTM_HEADS + h + 1] + bias_ref[:, MLSTM_HEADS + h:MLSTM_HEADS + h + 1]
    lf_col = _log_sigmoid(f_col)
    lf_row = _col_to_row(lf_col, eye)
    i_row = _col_to_row(i_col, eye)
    fcum_col = jnp.sum(jnp.where(causal, lf_row, 0.0), axis=1, keepdims=True)
    fcum_row = _col_to_row(fcum_col, eye)
    ftot_col = jnp.sum(jnp.where(same, lf_row, 0.0), axis=1, keepdims=True)
    return i_col, i_row, fcum_col, fcum_row, ftot_col


def _mlstm_prompt_kernel(q_ref, k_ref, v0_ref, v1_ref, o0_ref, o1_ref, zg_ref, bias_ref, gm_ref,
                         yb_ref, c_ref, n_ref, m_ref, c_scr, n_scr, m_scr):
    L = q_ref.shape[0]
    dk = q_ref.shape[1] // MLSTM_HEADS
    dv = 2 * v0_ref.shape[1] // MLSTM_HEADS
    step = pl.program_id(1)

    @pl.when(step == 0)
    def _():
        c_scr[...] = jnp.zeros_like(c_scr)
        n_scr[...] = jnp.zeros_like(n_scr)
        m_scr[...] = jnp.zeros_like(m_scr)

    ri = lax.broadcasted_iota(jnp.int32, (L, L), 0)
    ci = lax.broadcasted_iota(jnp.int32, (L, L), 1)
    causal = ci <= ri
    eye = ci == ri
    same = ci >= 0
    for h in range(MLSTM_HEADS):
        q = q_ref[:, h * dk:(h + 1) * dk] * (dk ** -0.5)
        k = k_ref[:, h * dk:(h + 1) * dk]
        v = _head_blocks((v0_ref, v1_ref), h, dv)
        o = _head_blocks((o0_ref, o1_ref), h, dv)
        i_col, i_row, fcum_col, fcum_row, ftot = _mlstm_gates(zg_ref, bias_ref, h, same, causal, eye)
        m_prev = m_scr[h:h + 1, 0:1]
        dlog = jnp.where(causal, fcum_col - fcum_row + i_row, -jnp.inf)
        m_inter = m_prev + fcum_col
        m_tok = jnp.maximum(m_inter, jnp.max(dlog, axis=1, keepdims=True))
        qb = q.astype(BF16)
        kb = k.astype(BF16)
        vb = v.astype(BF16)
        s = _dot_nt(qb, kb) * jnp.exp(dlog - m_tok)
        w_inter = jnp.exp(m_inter - m_tok)
        cmat = c_scr[h]
        nvec = n_scr[h:h + 1, :]
        num = _dot(s.astype(BF16), vb) + w_inter * _dot(qb, cmat.astype(BF16))
        den = jnp.sum(s, axis=1, keepdims=True) + w_inter * jnp.sum(q * nvec, axis=1, keepdims=True)
        den = jnp.maximum(jnp.abs(den), jnp.exp(-m_tok))
        hh = num / den
        ms = jnp.mean(hh * hh, axis=-1, keepdims=True)
        yn = (hh * lax.rsqrt(ms + EPS)) * gm_ref[:, h * dv:(h + 1) * dv]
        yb_ref[:, h * dv:(h + 1) * dv] = (yn * jax.nn.sigmoid(o)).astype(yb_ref.dtype)
        wlog = ftot - fcum_col + i_col
        m_new = jnp.maximum(m_prev + ftot[0:1, :], jnp.max(wlog, axis=0, keepdims=True))
        wk = jnp.exp(wlog - m_new)
        decay = jnp.exp(m_prev + ftot[0:1, :] - m_new)
        kw = k * wk
        c_scr[h] = decay * cmat + lax.dot_general(kw.astype(BF16), vb, (((0,), (0,)), ((), ())),
                                                  preferred_element_type=F32)
        n_scr[h:h + 1, :] = decay * nvec + jnp.sum(kw, axis=0, keepdims=True)
        m_scr[h:h + 1, :] = jnp.broadcast_to(m_new, (1, m_scr.shape[1]))

    @pl.when(step == pl.num_programs(1) - 1)
    def _():
        c_ref[0] = c_scr[...]
        n_ref[0] = n_scr[...]
        m_ref[0] = m_scr[0:MLSTM_HEADS, 0:1]


def _mlstm_prompt(z, zg, bias, g_m, batch, seq, q_blk, layer):
    m = z.shape[0]
    wb = g_m.shape[2]
    qw = wb // 2
    dk = qw // MLSTM_HEADS
    dv = wb // MLSTM_HEADS
    L = _pick(seq, (256, 128, 64, 32, 16, 8))
    nc = seq // L
    row = lambda b, c: b * nc + c
    zspec = lambda blk: pl.BlockSpec((L, qw), lambda b, c: (row(b, c), blk))
    est = 2 * 6 * L * qw * 4 + 2 * L * GATE_LANES * 4 + 2 * L * wb * 2 + 3 * MLSTM_HEADS * dk * dv * 4 + 16 * L * L * 4
    return pl.pallas_call(
        _mlstm_prompt_kernel,
        grid=(batch, nc),
        in_specs=[zspec(q_blk + i) for i in range(6)] + [
            pl.BlockSpec((L, GATE_LANES), lambda b, c: (row(b, c), 0)),
            _layer_spec(bias, layer),
            _layer_spec(g_m, layer),
        ],
        out_specs=[
            pl.BlockSpec((L, wb), lambda b, c: (row(b, c), 0)),
            pl.BlockSpec((1, MLSTM_HEADS, dk, dv), lambda b, c: (b, 0, 0, 0)),
            pl.BlockSpec((1, MLSTM_HEADS, dk), lambda b, c: (b, 0, 0)),
            pl.BlockSpec((1, MLSTM_HEADS, 1), lambda b, c: (b, 0, 0)),
        ],
        out_shape=[
            jax.ShapeDtypeStruct((m, wb), BF16),
            jax.ShapeDtypeStruct((batch, MLSTM_HEADS, dk, dv), F32),
            jax.ShapeDtypeStruct((batch, MLSTM_HEADS, dk), F32),
            jax.ShapeDtypeStruct((batch, MLSTM_HEADS, 1), F32),
        ],
        scratch_shapes=[
            pltpu.VMEM((MLSTM_HEADS, dk, dv), F32),
            pltpu.VMEM((MLSTM_HEADS, dk), F32),
            pltpu.VMEM((8, 128), F32),
        ],
        compiler_params=_params(("parallel", "arbitrary"), est),
    )(z, z, z, z, z, z, zg, bias, g_m)


def _mlstm_sample_kernel(q_ref, k_ref, v0_ref, v1_ref, o0_ref, o1_ref, zg_ref, bias_ref, gm_ref,
                         c0_ref, nrow_ref, mrow_ref, n0_ref, *rest, seq_rows):
    yb_ref, c_ref, n_ref, m_ref = rest[-4:]
    R = q_ref.shape[0]
    nb = R // seq_rows
    dk = q_ref.shape[1] // MLSTM_HEADS
    dv = 2 * v0_ref.shape[1] // MLSTM_HEADS
    ri = lax.broadcasted_iota(jnp.int32, (R, R), 0)
    ci = lax.broadcasted_iota(jnp.int32, (R, R), 1)
    same = _seq_of(ri, seq_rows) == _seq_of(ci, seq_rows)
    causal = same & (ci <= ri)
    eye = ci == ri
    rid = lax.broadcasted_iota(jnp.int32, (R, 1), 0)
    for h in range(MLSTM_HEADS):
        q = q_ref[:, h * dk:(h + 1) * dk] * (dk ** -0.5)
        k = k_ref[:, h * dk:(h + 1) * dk]
        v = _head_blocks((v0_ref, v1_ref), h, dv)
        o = _head_blocks((o0_ref, o1_ref), h, dv)
        i_col, i_row, fcum_col, fcum_row, ftot = _mlstm_gates(zg_ref, bias_ref, h, same, causal, eye)
        m_prev = mrow_ref[:, h:h + 1]
        dlog = jnp.where(causal, fcum_col - fcum_row + i_row, -jnp.inf)
        m_inter = m_prev + fcum_col
        m_tok = jnp.maximum(m_inter, jnp.max(dlog, axis=1, keepdims=True))
        qb = q.astype(BF16)
        kb = k.astype(BF16)
        vb = v.astype(BF16)
        s = _dot_nt(qb, kb) * jnp.exp(dlog - m_tok)
        w_inter = jnp.exp(m_inter - m_tok)
        inter = jnp.zeros((R, dv), F32)
        for b in range(nb):
            mine = _seq_of(rid, seq_rows) == b
            inter = inter + jnp.where(mine, _dot(qb, c0_ref[b, h].astype(BF16)), 0.0)
        num = _dot(s.astype(BF16), vb) + w_inter * inter
        nrow = nrow_ref[:, h * dk:(h + 1) * dk]
        den = jnp.sum(s, axis=1, keepdims=True) + w_inter * jnp.sum(q * nrow, axis=1, keepdims=True)
        den = jnp.maximum(jnp.abs(den), jnp.exp(-m_tok))
        hh = num / den
        ms = jnp.mean(hh * hh, axis=-1, keepdims=True)
        yn = (hh * lax.rsqrt(ms + EPS)) * gm_ref[:, h * dv:(h + 1) * dv]
        yb_ref[:, h * dv:(h + 1) * dv] = (yn * jax.nn.sigmoid(o)).astype(yb_ref.dtype)
        wlog_col = ftot - fcum_col + i_col
        wlog_row = _col_to_row(wlog_col, eye)
        m_new = jnp.maximum(m_prev + ftot, jnp.max(jnp.where(same, wlog_row, -jnp.inf), axis=1, keepdims=True))
        wk = jnp.exp(wlog_col - m_new)
        decay = jnp.exp(m_prev + ftot - m_new)
        kw = k * wk
        for b in range(nb):
            mine = _seq_of(rid, seq_rows) == b
            kw_b = jnp.where(mine, kw, 0.0)
            r0 = b * seq_rows
            dec_b = decay[r0:r0 + 1, :]
            upd = lax.dot_general(kw_b.astype(BF16), vb, (((0,), (0,)), ((), ())), preferred_element_type=F32)
            c_ref[b, h] = dec_b * c0_ref[b, h] + upd
            n_ref[b, h:h + 1, :] = dec_b * n0_ref[b, h:h + 1, :] + jnp.sum(kw_b, axis=0, keepdims=True)
            m_ref[0, b:b + 1, h:h + 1] = m_new[r0:r0 + 1, :]


def _mlstm_sample(z, zg, bias, g_m, c0, n0, n_rows, m_rows, c_acc, seq_rows, q_blk, layer):
    m = z.shape[0]
    depth, batch = c0.shape[:2]
    wb = g_m.shape[2]
    qw = wb // 2
    dk = qw // MLSTM_HEADS
    dv = wb // MLSTM_HEADS
    nb = 4
    R = nb * seq_rows
    zspec = lambda blk: pl.BlockSpec((R, qw), lambda s: (s, blk))
    cblk = nb * MLSTM_HEADS * dk * dv * 4
    est = 4 * cblk + 2 * 7 * R * qw * 4 + 2 * R * GATE_LANES * 4 + 2 * R * wb * 2
    in_specs = [zspec(q_blk + i) for i in range(6)] + [
        pl.BlockSpec((R, GATE_LANES), lambda s: (s, 0)),
        _layer_spec(bias, layer),
        _layer_spec(g_m, layer),
        pl.BlockSpec((None, nb, MLSTM_HEADS, dk, dv), lambda s: (layer, s, 0, 0, 0)),
        pl.BlockSpec((None, R, qw), lambda s: (layer, s, 0)),
        pl.BlockSpec((None, R, MLSTM_HEADS), lambda s: (layer, s, 0)),
        pl.BlockSpec((None, nb, MLSTM_HEADS, dk), lambda s: (layer, s, 0, 0)),
    ]
    args = [z, z, z, z, z, z, zg, bias, g_m, c0, n_rows, m_rows, n0]
    aliases = {}
    if c_acc is not None:
        in_specs.append(pl.BlockSpec(memory_space=pl.ANY))
        aliases = {len(args): 1}
        args.append(c_acc)
    yb, c_all, n1, m1 = pl.pallas_call(
        functools.partial(_mlstm_sample_kernel, seq_rows=seq_rows),
        grid=(m // R,),
        in_specs=in_specs,
        out_specs=[
            pl.BlockSpec((R, wb), lambda s: (s, 0)),
            pl.BlockSpec((None, nb, MLSTM_HEADS, dk, dv), lambda s: (layer, s, 0, 0, 0)),
            pl.BlockSpec((nb, MLSTM_HEADS, dk), lambda s: (s, 0, 0)),
            pl.BlockSpec((1, nb, MLSTM_HEADS), lambda s: (s, 0, 0)),
        ],
        out_shape=[
            jax.ShapeDtypeStruct((m, wb), BF16),
            jax.ShapeDtypeStruct(c0.shape, F32),
            jax.ShapeDtypeStruct(n0.shape[1:], F32),
            jax.ShapeDtypeStruct((batch // nb, nb, MLSTM_HEADS), F32),
        ],
        input_output_aliases=aliases,
        compiler_params=_params(("parallel",), est),
    )(*args)
    return yb, c_all, n1, m1.reshape(batch, MLSTM_HEADS)


def kernel(x_prompt, x_sample, state_pool, state_mlstm_c, state_mlstm_n, state_mlstm_m, p_prompt, p_sample, g_mix, w_in, b_igate, b_fgate, w_pool, pool_scale, g_mlstm, w_s, b_s, g_v, w_out, g_ffn, w_ffn_gate, w_ffn_up, w_ffn_down, g_ple, w_ple_gate, w_ple_proj, g_final):
    batch, seq, d = x_prompt.shape
    dec_batch, dec_seq, _ = x_sample.shape
    depth = w_in.shape[0]
    wa = pool_scale.shape[1]
    wb = g_mlstm.shape[1]
    wc = g_v.shape[1]
    n_gates = 2 * MLSTM_HEADS
    off_gate = wa + 2 * (wb // 2) + 2 * wb
    off_c = off_gate + n_gates
    q_blk = wa // (wb // 2)
    u_blk = off_gate // wc

    w_in_t = jnp.swapaxes(w_in, 1, 2) * g_mix[:, None, :]
    wt_main = jnp.concatenate([w_in_t[:, :off_gate], w_in_t[:, off_c:]], axis=1).astype(BF16)
    wt_gate = jnp.pad(w_in_t[:, off_gate:off_c], ((0, 0), (0, GATE_LANES - n_gates), (0, 0))).astype(BF16)
    gate_bias = jnp.pad(jnp.concatenate([b_igate, b_fgate], axis=1), ((0, 0), (0, GATE_LANES - n_gates)))[:, None, :]
    w_pool_b = w_pool.astype(BF16)
    w_out_b = w_out.astype(BF16)
    w_fg = (w_ffn_gate * g_ffn[:, :, None]).astype(BF16)
    w_fu = (w_ffn_up * g_ffn[:, :, None]).astype(BF16)
    w_fd = w_ffn_down.astype(BF16)
    w_pg = (w_ple_gate * g_ple[:, :, None]).astype(BF16)
    w_pp = w_ple_proj.astype(BF16)
    reps = CHUNK // dec_seq
    ws_sample = jnp.tile(w_s[:, :, :dec_seq, :dec_seq], (1, 1, reps, reps))
    bias_prompt = jnp.swapaxes(b_s, 1, 2)
    bias_sample = jnp.tile(jnp.swapaxes(b_s[:, :, :dec_seq], 1, 2), (1, reps, 1))
    vec = lambda a: a[:, None, :]
    g_m3, g_v3, scale3 = map(vec, (g_mlstm, g_v, pool_scale))

    pp = p_prompt.reshape(depth, batch * seq, -1)
    ps = p_sample.reshape(depth, dec_batch * dec_seq, -1)
    hist_tm = jnp.swapaxes(state_pool, 1, 2)
    n_rows = jnp.repeat(state_mlstm_n.reshape(depth, dec_batch, -1), dec_seq, axis=1)
    m_rows = jnp.repeat(state_mlstm_m, dec_seq, axis=1)

    xp = x_prompt.reshape(batch * seq, d)
    xs = x_sample.reshape(dec_batch * dec_seq, d)
    xpb, ssp = _prep_rows(xp)
    xsb, sss = _prep_rows(xs)
    v_start = ((seq - 1) // CHUNK) * CHUNK
    outs = {k: [] for k in ("pool_p", "pool_s", "c_p", "n_p", "n_s", "m_p", "m_s", "v_p", "v_s")}
    c_s = None

    for i in range(depth):
        zp, zgp = _inproj(xpb, ssp, wt_main, wt_gate, i)
        ya = _pool_prompt(zp, seq, w_pool_b, scale3, i)
        yb, c1, n1, m1 = _mlstm_prompt(zp, zgp, gate_bias, g_m3, batch, seq, q_blk, i)
        yc, vn = _chunk_mlp(zp, u_blk, w_s, bias_prompt, g_v3, CHUNK, i)
        xp, xpb, ssp = _wout(ya, yb, yc, w_out_b, xp, i)
        xp, xpb, ssp = _ffn_down(_ffn_up(xpb, ssp, w_fg, w_fu, i), w_fd, xp, i)
        xp, xpb, ssp = _ple(xpb, ssp, w_pg, pp, w_pp, xp, i)
        outs["pool_p"].append(zp.reshape(batch, seq, -1)[:, seq - POOL_BUF:, :wa])
        outs["c_p"].append(c1)
        outs["n_p"].append(n1)
        outs["m_p"].append(m1.reshape(batch, MLSTM_HEADS))
        outs["v_p"].append(vn.reshape(batch, seq, wc)[:, v_start:])
        zs, zgs = _inproj(xsb, sss, wt_main, wt_gate, i)
        new_tm = jnp.swapaxes(zs[:, :wa].reshape(dec_batch, dec_seq, wa), 0, 1)
        ya_tm = _pool_sample(hist_tm, new_tm, w_pool_b, scale3, PAST_LEN, i)
        ya = jnp.swapaxes(ya_tm, 0, 1).reshape(dec_batch * dec_seq, wa)
        yb, c_s, n1, m1 = _mlstm_sample(zs, zgs, gate_bias, g_m3, state_mlstm_c, state_mlstm_n, n_rows, m_rows,
                                        c_s, dec_seq, q_blk, i)
        yc, vn = _chunk_mlp(zs, u_blk, ws_sample, bias_sample, g_v3, dec_seq, i)
        xs, xsb, sss = _wout(ya, yb, yc, w_out_b, xs, i)
        xs, xsb, sss = _ffn_down(_ffn_up(xsb, sss, w_fg, w_fu, i), w_fd, xs, i)
        xs, xsb, sss = _ple(xsb, sss, w_pg, ps, w_pp, xs, i)
        outs["pool_s"].append(jnp.concatenate([hist_tm[i], new_tm], axis=0)[dec_seq:])
        outs["n_s"].append(n1)
        outs["m_s"].append(m1)
        outs["v_s"].append(vn.reshape(dec_batch, dec_seq, wc))

    y_prompt = _final_norm(xp, g_final[None, :]).reshape(batch, seq, d)
    y_sample = _final_norm(xs, g_final[None, :]).reshape(dec_batch, dec_seq, d)
    st = lambda k: jnp.stack(outs[k])
    pool_sample = jnp.swapaxes(st("pool_s"), 1, 2)
    return (y_prompt, y_sample, st("pool_p"), pool_sample, st("c_p"), c_s, st("n_p"), st("n_s"),
            st("m_p"), st("m_s"), st("v_p"), st("v_s"))
```

```python
import functools
import math

import jax
import jax.numpy as jnp
from jax import lax
from jax.experimental import pallas as pl
from jax.experimental.pallas import tpu as pltpu

F32 = jnp.float32
BF16 = jnp.bfloat16

EPS = 1e-6
PAST_LEN = 16384
POOL_WINDOWS = (2, 4, 8, 16)
POOL_BUF = max(POOL_WINDOWS) - 1
POOL_HALO = 16
N_POOL = len(POOL_WINDOWS)
MLSTM_HEADS = 4
N_CM = 4
CHUNK = 128
GATE_LANES = 128
SS_LANES = 128

VMEM_SLACK_BYTES = 10 * 2**20
VMEM_CAP_BYTES = 58 * 2**20

_ROW_TILES = (1024, 512, 256, 128, 64, 32, 16, 8)


def _params(sem, vmem_estimate):
    limit = min(int(vmem_estimate) + VMEM_SLACK_BYTES, VMEM_CAP_BYTES)
    return pltpu.CompilerParams(dimension_semantics=sem, vmem_limit_bytes=limit)


def _pick(n, candidates):
    for c in candidates:
        if n % c == 0:
            return c
    return n


def _single(shape, index_map):
    return pl.BlockSpec(shape, index_map, pipeline_mode=pl.Buffered(1))


def _layer_spec(arr, layer):
    nd = arr.ndim - 1
    return pl.BlockSpec((None,) + arr.shape[1:], lambda *_: (layer,) + (0,) * nd)


def _dot(a, b):
    return jnp.dot(a, b, preferred_element_type=F32)


def _dot_nt(a, bt):
    return lax.dot_general(a, bt, (((1,), (1,)), ((), ())), preferred_element_type=F32)


def _gelu_tanh(x):
    return 0.5 * x * (1.0 + jnp.tanh(math.sqrt(2.0 / math.pi) * (x + 0.044715 * (x * x * x))))


def _seq_of(idx, seq_rows):
    if seq_rows & (seq_rows - 1) == 0:
        return lax.shift_right_logical(idx, seq_rows.bit_length() - 1)
    return idx // seq_rows


def _log_sigmoid(x):
    return jnp.minimum(x, 0.0) - jnp.log(1.0 + jnp.exp(-jnp.abs(x)))


def _row_scale(ss_ref, width):
    return lax.rsqrt(ss_ref[:, 0:1] * (1.0 / width) + EPS)


def _gain_spec(bn, g_idx):
    return pl.BlockSpec((None, 1, bn), lambda i, j: (g_idx, 0, j))


def _emit_rows(xn, j, gn_ref, o_ref, xb_ref, ss_ref):
    o_ref[...] = xn
    xb_ref[...] = (xn * gn_ref[...]).astype(xb_ref.dtype)
    part = jnp.broadcast_to(jnp.sum(xn * xn, axis=-1, keepdims=True), ss_ref.shape)

    @pl.when(j == 0)
    def _():
        ss_ref[...] = part

    @pl.when(j > 0)
    def _():
        ss_ref[...] += part


def _row_outs(m, d, bm, bn):
    specs = [
        pl.BlockSpec((bm, bn), lambda i, j: (i, j)),
        pl.BlockSpec((bm, bn), lambda i, j: (i, j)),
        pl.BlockSpec((bm, SS_LANES), lambda i, j: (i, 0)),
    ]
    shapes = [jax.ShapeDtypeStruct((m, d), F32), jax.ShapeDtypeStruct((m, d), BF16),
              jax.ShapeDtypeStruct((m, SS_LANES), F32)]
    return specs, shapes


def _prep_rows_kernel(x_ref, gn_ref, xb_ref, ss_ref):
    x = x_ref[...]
    xb_ref[...] = (x * gn_ref[...]).astype(xb_ref.dtype)
    ss_ref[...] = jnp.broadcast_to(jnp.sum(x * x, axis=-1, keepdims=True), ss_ref.shape)


def _prep_rows(x, g, g_idx):
    m, d = x.shape
    bm = _pick(m, (256, 128, 64, 32, 16, 8))
    return pl.pallas_call(
        _prep_rows_kernel,
        grid=(m // bm,),
        in_specs=[pl.BlockSpec((bm, d), lambda i: (i, 0)), _layer_spec(g, g_idx)],
        out_specs=[pl.BlockSpec((bm, d), lambda i: (i, 0)), pl.BlockSpec((bm, SS_LANES), lambda i: (i, 0))],
        out_shape=[jax.ShapeDtypeStruct((m, d), BF16), jax.ShapeDtypeStruct((m, SS_LANES), F32)],
        compiler_params=_params(("parallel",), 2 * bm * d * 6),
    )(x, g)


def _inproj_kernel(xb_ref, ss_ref, wt_ref, wgt_ref, z_ref, zg_ref):
    rs = _row_scale(ss_ref, xb_ref.shape[1])

    @pl.when(pl.program_id(1) == 0)
    def _():
        zg_ref[...] = rs * _dot_nt(xb_ref[...], wgt_ref[...].astype(BF16))

    z_ref[...] = rs * _dot_nt(xb_ref[...], wt_ref[0].astype(BF16))


def _inproj(xb, ss, wt, wt_gate, n_lead, n_skip, layer):
    m, d = xb.shape
    n = wt.shape[1] - n_skip
    bm = _pick(m, _ROW_TILES)
    bn = _pick(math.gcd(n_lead, n), (512, 256, 128))
    lead_blocks = n_lead // bn
    assert n_skip % 8 == 0 and n_lead % bn == 0 and n % bn == 0
    est = (2 * bm * d * 2 + 2 * d * bn * 4 + 2 * d * GATE_LANES * 4 + 2 * bm * bn * 4 + 4 * bm * GATE_LANES * 4
           + d * bn * 2)
    return pl.pallas_call(
        _inproj_kernel,
        grid=(m // bm, n // bn),
        in_specs=[
            pl.BlockSpec((bm, d), lambda i, j: (i, 0)),
            pl.BlockSpec((bm, SS_LANES), lambda i, j: (i, 0)),
            pl.BlockSpec((pl.Element(1), pl.Element(bn), pl.Element(d)),
                         lambda i, j: (layer, pl.multiple_of(j * bn + jnp.where(j >= lead_blocks, n_skip, 0), 8), 0)),
            _layer_spec(wt_gate, layer),
        ],
        out_specs=[
            pl.BlockSpec((bm, bn), lambda i, j: (i, j)),
            pl.BlockSpec((bm, GATE_LANES), lambda i, j: (i, 0)),
        ],
        out_shape=[jax.ShapeDtypeStruct((m, n), F32), jax.ShapeDtypeStruct((m, GATE_LANES), F32)],
        compiler_params=_params(("parallel", "arbitrary"), est),
    )(xb, ss, wt, wt_gate)


def _ffn_up_kernel(xb_ref, ss_ref, wg_ref, wu_ref, o_ref):
    rs = _row_scale(ss_ref, xb_ref.shape[1])
    h = xb_ref[...]
    a = rs * _dot(h, wg_ref[...].astype(BF16))
    b = rs * _dot(h, wu_ref[...].astype(BF16))
    o_ref[...] = ((a * jax.nn.sigmoid(a)) * b).astype(o_ref.dtype)


def _ffn_up(xb, ss, w_gate, w_up, layer):
    m, d = xb.shape
    n = w_gate.shape[2]
    bm = _pick(m, _ROW_TILES)
    tb = _pick(n, (256, 128))
    est = 2 * bm * d * 2 + 4 * d * tb * 4 + 2 * d * tb * 2 + 2 * bm * tb * 2 + 2 * bm * SS_LANES * 4
    wspec = pl.BlockSpec((None, d, tb), lambda i, j: (layer, 0, j))
    return pl.pallas_call(
        _ffn_up_kernel,
        grid=(m // bm, n // tb),
        in_specs=[pl.BlockSpec((bm, d), lambda i, j: (i, 0)), pl.BlockSpec((bm, SS_LANES), lambda i, j: (i, 0)),
                  wspec, wspec],
        out_specs=pl.BlockSpec((bm, tb), lambda i, j: (i, j)),
        out_shape=jax.ShapeDtypeStruct((m, n), BF16),
        compiler_params=_params(("parallel", "arbitrary"), est),
    )(xb, ss, w_gate, w_up)


def _ple_kernel(xb_ref, ss_ref, w_ref, p_ref, wpp_ref, x_ref, gn_ref, o_ref, xbo_ref, sso_ref):
    rs = _row_scale(ss_ref, xb_ref.shape[1])
    gate = jax.nn.sigmoid(rs * _dot(xb_ref[...], w_ref[...].astype(BF16)))
    emb = _dot(p_ref[...].astype(BF16), wpp_ref[...].astype(BF16))
    _emit_rows(x_ref[...] + gate * emb, pl.program_id(1), gn_ref, o_ref, xbo_ref, sso_ref)


def _ple(xb, ss, w_pg, p, w_pp, x, layer, g_next, g_idx):
    m, d = x.shape
    pd = p.shape[2]
    bm = _pick(m, _ROW_TILES)
    bn = _pick(d, (512, 256, 128))
    est = (2 * bm * d * 2 + 2 * d * bn * 4 + d * bn * 2 + 2 * bm * pd * 4 + 2 * pd * bn * 4 + 4 * bm * bn * 4
           + 2 * bm * bn * 2 + 4 * bm * SS_LANES * 4)
    out_specs, out_shape = _row_outs(m, d, bm, bn)
    return pl.pallas_call(
        _ple_kernel,
        grid=(m // bm, d // bn),
        in_specs=[
            pl.BlockSpec((bm, d), lambda i, j: (i, 0)),
            pl.BlockSpec((bm, SS_LANES), lambda i, j: (i, 0)),
            pl.BlockSpec((None, d, bn), lambda i, j: (layer, 0, j)),
            pl.BlockSpec((None, bm, pd), lambda i, j: (layer, i, 0)),
            pl.BlockSpec((None, pd, bn), lambda i, j: (layer, 0, j)),
            pl.BlockSpec((bm, bn), lambda i, j: (i, j)),
            _gain_spec(bn, g_idx),
        ],
        out_specs=out_specs,
        out_shape=out_shape,
        compiler_params=_params(("parallel", "arbitrary"), est),
    )(xb, ss, w_pg, p, w_pp, x, g_next)


def _wout_kernel(ya_ref, yb_ref, yc_ref, w_ref, x_ref, gn_ref, o_ref, xbo_ref, sso_ref, a_scr):
    j = pl.program_id(1)

    @pl.when(j == 0)
    def _():
        wa = ya_ref.shape[1]
        wb = yb_ref.shape[1]
        a_scr[:, :wa] = ya_ref[...].astype(a_scr.dtype)
        a_scr[:, wa:wa + wb] = yb_ref[...].astype(a_scr.dtype)
        a_scr[:, wa + wb:] = yc_ref[...].astype(a_scr.dtype)

    _emit_rows(x_ref[...] + _dot(a_scr[...], w_ref[...].astype(BF16)), j, gn_ref, o_ref, xbo_ref, sso_ref)


def _wout(ya, yb, yc, w, x, layer, g_next, g_idx):
    m, d = x.shape
    wa, wb, wc = ya.shape[1], yb.shape[1], yc.shape[1]
    k = wa + wb + wc
    bm = _pick(m, _ROW_TILES)
    bn = _pick(d, (512, 256, 128))
    est = (bm * wa * ya.dtype.itemsize + bm * wb * 2 + bm * wc * 2 + bm * k * 2
           + 2 * k * bn * 4 + k * bn * 2 + 4 * bm * bn * 4 + 2 * bm * bn * 2 + 2 * bm * SS_LANES * 4)
    out_specs, out_shape = _row_outs(m, d, bm, bn)
    return pl.pallas_call(
        _wout_kernel,
        grid=(m // bm, d // bn),
        in_specs=[
            _single((bm, wa), lambda i, j: (i, 0)),
            _single((bm, wb), lambda i, j: (i, 0)),
            _single((bm, wc), lambda i, j: (i, 0)),
            pl.BlockSpec((None, k, bn), lambda i, j: (layer, 0, j)),
            pl.BlockSpec((bm, bn), lambda i, j: (i, j)),
            _gain_spec(bn, g_idx),
        ],
        out_specs=out_specs,
        out_shape=out_shape,
        scratch_shapes=[pltpu.VMEM((bm, k), BF16)],
        compiler_params=_params(("parallel", "arbitrary"), est),
    )(ya, yb, yc, w, x, g_next)


def _ffn_down_kernel(a_ref, w_ref, x_ref, gn_ref, o_ref, xbo_ref, sso_ref):
    _emit_rows(x_ref[...] + _dot(a_ref[...], w_ref[...]), pl.program_id(1), gn_ref, o_ref, xbo_ref, sso_ref)


def _ffn_down(a, w, x, layer, g_next, g_idx):
    m, d = x.shape
    k = a.shape[1]
    bm = _pick(m, _ROW_TILES)
    bn = _pick(d, (256, 128))
    est = bm * k * 2 + 2 * k * bn * 2 + 4 * bm * bn * 4 + 2 * bm * bn * 2 + 2 * bm * SS_LANES * 4
    out_specs, out_shape = _row_outs(m, d, bm, bn)
    return pl.pallas_call(
        _ffn_down_kernel,
        grid=(m // bm, d // bn),
        in_specs=[
            _single((bm, k), lambda i, j: (i, 0)),
            pl.BlockSpec((None, k, bn), lambda i, j: (layer, 0, j)),
            pl.BlockSpec((bm, bn), lambda i, j: (i, j)),
            _gain_spec(bn, g_idx),
        ],
        out_specs=out_specs,
        out_shape=out_shape,
        compiler_params=_params(("parallel", "arbitrary"), est),
    )(a, w, x, g_next)


def _final_norm_kernel(x_ref, g_ref, o_ref):
    x = x_ref[...]
    ms = jnp.mean(x * x, axis=-1, keepdims=True)
    o_ref[...] = (x * lax.rsqrt(ms + EPS)) * g_ref[...]


def _final_norm(x, g):
    m, d = x.shape
    bm = _pick(m, (256, 128, 64, 32, 16, 8))
    return pl.pallas_call(
        _final_norm_kernel,
        grid=(m // bm,),
        in_specs=[pl.BlockSpec((bm, d), lambda i: (i, 0)), pl.BlockSpec((1, d), lambda i: (0, 0))],
        out_specs=pl.BlockSpec((bm, d), lambda i: (i, 0)),
        out_shape=jax.ShapeDtypeStruct((m, d), F32),
        compiler_params=_params(("parallel",), 4 * bm * d * 4),
    )(x, g)


def _pool_prompt_kernel(a_ref, halo_ref, wp_ref, sc_ref, y_ref, ext_scr, *, tiles_per_seq):
    tp = a_ref.shape[0]
    cg = wp_ref.shape[1]
    tile = pl.program_id(0) % tiles_per_seq
    ext_scr[0:POOL_HALO, :] = jnp.where(tile == 0, 0.0, halo_ref[...])
    ext_scr[POOL_HALO:, :] = a_ref[...]
    pos = tile * tp + lax.broadcasted_iota(jnp.int32, (tp, 1), 0)
    for g, w in enumerate(POOL_WINDOWS):
        sl = slice(g * cg, (g + 1) * cg)
        s = ext_scr[POOL_HALO:POOL_HALO + tp, sl]
        for k in range(1, w):
            s = s + ext_scr[POOL_HALO - k:POOL_HALO - k + tp, sl]
        cnt = jnp.minimum(pos + 1, w).astype(F32)
        dlt = s / cnt - a_ref[:, sl]
        y = _dot(dlt.astype(BF16), wp_ref[g]) * sc_ref[:, sl]
        y_ref[:, sl] = y.astype(y_ref.dtype)


def _pool_prompt(z, seq, w_pool, scale, layer):
    m = z.shape[0]
    wa = scale.shape[2]
    tp = _pick(seq, (512, 256, 128, 64, 32, 16))
    tiles_per_seq = seq // tp
    halo_blocks = tp // POOL_HALO
    est = (2 * tp * wa * 4 + 2 * POOL_HALO * wa * 4 + (tp + POOL_HALO) * wa * 4 + 2 * tp * wa * 2
           + 4 * wa * wa // N_POOL)
    return pl.pallas_call(
        functools.partial(_pool_prompt_kernel, tiles_per_seq=tiles_per_seq),
        grid=(m // tp,),
        in_specs=[
            pl.BlockSpec((tp, wa), lambda i: (i, 0)),
            pl.BlockSpec((POOL_HALO, wa), lambda i: (jnp.maximum(i * halo_blocks - 1, 0), 0)),
            _layer_spec(w_pool, layer),
            _layer_spec(scale, layer),
        ],
        out_specs=pl.BlockSpec((tp, wa), lambda i: (i, 0)),
        out_shape=jax.ShapeDtypeStruct((m, wa), BF16),
        scratch_shapes=[pltpu.VMEM((tp + POOL_HALO, wa), F32)],
        compiler_params=_params(("parallel",), est),
    )(z, z, w_pool, scale)


def _pool_sample_kernel(hist_ref, new_ref, wp_ref, sc_ref, y_ref, *, first_pos):
    cg = wp_ref.shape[1]
    n_new = new_ref.shape[0]
    rows = [hist_ref[r] for r in range(POOL_BUF)] + [new_ref[t] for t in range(n_new)]
    for t in range(n_new):
        for g, w in enumerate(POOL_WINDOWS):
            sl = slice(g * cg, (g + 1) * cg)
            s = rows[POOL_BUF + t][:, sl]
            for k in range(1, w):
                s = s + rows[POOL_BUF + t - k][:, sl]
            cnt = float(min(first_pos + t + 1, w))
            dlt = s / cnt - rows[POOL_BUF + t][:, sl]
            y = _dot(dlt.astype(BF16), wp_ref[g]) * sc_ref[:, sl]
            y_ref[t, :, sl] = y.astype(y_ref.dtype)


def _pool_sample(hist, new, w_pool, scale, first_pos, layer):
    n_new, b, wa = new.shape
    bb = _pick(b, (32, 16, 8))
    est = 2 * (POOL_BUF + 2 * n_new) * bb * wa * 4 + 4 * wa * wa // N_POOL
    return pl.pallas_call(
        functools.partial(_pool_sample_kernel, first_pos=first_pos),
        grid=(b // bb,),
        in_specs=[
            pl.BlockSpec((None, POOL_BUF, bb, wa), lambda i: (layer, 0, i, 0)),
            pl.BlockSpec((n_new, bb, wa), lambda i: (0, i, 0)),
            _layer_spec(w_pool, layer),
            _layer_spec(scale, layer),
        ],
        out_specs=pl.BlockSpec((n_new, bb, wa), lambda i: (0, i, 0)),
        out_shape=jax.ShapeDtypeStruct((n_new, b, wa), F32),
        compiler_params=_params(("parallel",), est),
    )(hist, new, w_pool, scale)


def _chunk_mlp_kernel(u_ref, v_ref, ws_ref, b_ref, gv_ref, yc_ref, vn_ref, *, seq_rows):
    rows = u_ref.shape[0]
    cg = u_ref.shape[1] // N_CM
    ri = lax.broadcasted_iota(jnp.int32, (rows, rows), 0)
    ci = lax.broadcasted_iota(jnp.int32, (rows, rows), 1)
    mask = ci <= ri
    if seq_rows < rows:
        mask = mask & (_seq_of(ri, seq_rows) == _seq_of(ci, seq_rows))
    for g in range(N_CM):
        sl = slice(g * cg, (g + 1) * cg)
        u = _gelu_tanh(u_ref[:, sl])
        v = _gelu_tanh(v_ref[:, sl])
        ms = jnp.mean(v * v, axis=-1, keepdims=True)
        vn = (v * lax.rsqrt(ms + EPS)) * gv_ref[:, sl]
        vn_ref[:, sl] = vn
        wm = jnp.where(mask, ws_ref[g], 0.0).astype(BF16)
        mix = _dot(wm, vn.astype(BF16)) + b_ref[:, g:g + 1]
        yc_ref[:, sl] = (u * mix).astype(yc_ref.dtype)


def _chunk_mlp(z, u_blk, w_mix, bias, g_v, seq_rows, layer):
    m = z.shape[0]
    wc = g_v.shape[2]
    est = 4 * CHUNK * wc * 4 + 2 * N_CM * CHUNK * CHUNK * 4 + 2 * CHUNK * wc * 2 + 2 * CHUNK * wc * 4
    return pl.pallas_call(
        functools.partial(_chunk_mlp_kernel, seq_rows=seq_rows),
        grid=(m // CHUNK,),
        in_specs=[
            pl.BlockSpec((CHUNK, wc), lambda i: (i, u_blk)),
            pl.BlockSpec((CHUNK, wc), lambda i: (i, u_blk + 1)),
            _layer_spec(w_mix, layer),
            _layer_spec(bias, layer),
            _layer_spec(g_v, layer),
        ],
        out_specs=[pl.BlockSpec((CHUNK, wc), lambda i: (i, 0)), pl.BlockSpec((CHUNK, wc), lambda i: (i, 0))],
        out_shape=[jax.ShapeDtypeStruct((m, wc), BF16), jax.ShapeDtypeStruct((m, wc), F32)],
        compiler_params=_params(("parallel",), est),
    )(z, z, w_mix, bias, g_v)


def _col_to_row(col, eye):
    return jnp.sum(jnp.where(eye, col, 0.0), axis=0, keepdims=True)


def _head_blocks(refs, h, width):
    r = refs[h // 2]
    return r[:, (h % 2) * width:(h % 2 + 1) * width]


def _mlstm_gates(zg_ref, bias_ref, h, same, causal, eye):
    i_col = zg_ref[:, h:h + 1] + bias_ref[:, h:h + 1]
    f_col = zg_ref[:, MLSTM_HEADS + h:MLSTM_HEADS + h + 1] + bias_ref[:, MLSTM_HEADS + h:MLSTM_HEADS + h + 1]
    lf_col = _log_sigmoid(f_col)
    lf_row = _col_to_row(lf_col, eye)
    i_row = _col_to_row(i_col, eye)
    fcum_col = jnp.sum(jnp.where(causal, lf_row, 0.0), axis=1, keepdims=True)
    fcum_row = _col_to_row(fcum_col, eye)
    ftot_col = jnp.sum(jnp.where(same, lf_row, 0.0), axis=1, keepdims=True)
    return i_col, i_row, fcum_col, fcum_row, ftot_col


def _mlstm_prompt_kernel(q_ref, k_ref, v0_ref, v1_ref, o0_ref, o1_ref, zg_ref, bias_ref, gm_ref,
                         yb_ref, c_ref, n_ref, m_ref, c_scr, n_scr, m_scr):
    L = q_ref.shape[0]
    dk = q_ref.shape[1] // MLSTM_HEADS
    dv = 2 * v0_ref.shape[1] // MLSTM_HEADS
    step = pl.program_id(1)

    @pl.when(step == 0)
    def _():
        c_scr[...] = jnp.zeros_like(c_scr)
        n_scr[...] = jnp.zeros_like(n_scr)
        m_scr[...] = jnp.zeros_like(m_scr)

    ri = lax.broadcasted_iota(jnp.int32, (L, L), 0)
    ci = lax.broadcasted_iota(jnp.int32, (L, L), 1)
    causal = ci <= ri
    eye = ci == ri
    same = ci >= 0
    for h in range(MLSTM_HEADS):
        q = q_ref[:, h * dk:(h + 1) * dk] * (dk ** -0.5)
        k = k_ref[:, h * dk:(h + 1) * dk]
        v = _head_blocks((v0_ref, v1_ref), h, dv)
        o = _head_blocks((o0_ref, o1_ref), h, dv)
        i_col, i_row, fcum_col, fcum_row, ftot = _mlstm_gates(zg_ref, bias_ref, h, same, causal, eye)
        m_prev = m_scr[h:h + 1, 0:1]
        dlog = jnp.where(causal, fcum_col - fcum_row + i_row, -jnp.inf)
        m_inter = m_prev + fcum_col
        m_tok = jnp.maximum(m_inter, jnp.max(dlog, axis=1, keepdims=True))
        qb = q.astype(BF16)
        kb = k.astype(BF16)
        vb = v.astype(BF16)
        s = _dot_nt(qb, kb) * jnp.exp(dlog - m_tok)
        w_inter = jnp.exp(m_inter - m_tok)
        cmat = c_scr[h]
        nvec = n_scr[h:h + 1, :]
        num = _dot(s.astype(BF16), vb) + w_inter * _dot(qb, cmat.astype(BF16))
        den = jnp.sum(s, axis=1, keepdims=True) + w_inter * jnp.sum(q * nvec, axis=1, keepdims=True)
        den = jnp.maximum(jnp.abs(den), jnp.exp(-m_tok))
        hh = num / den
        ms = jnp.mean(hh * hh, axis=-1, keepdims=True)
        yn = (hh * lax.rsqrt(ms + EPS)) * gm_ref[:, h * dv:(h + 1) * dv]
        yb_ref[:, h * dv:(h + 1) * dv] = (yn * jax.nn.sigmoid(o)).astype(yb_ref.dtype)
        wlog = ftot - fcum_col + i_col
        m_new = jnp.maximum(m_prev + ftot[0:1, :], jnp.max(wlog, axis=0, keepdims=True))
        wk = jnp.exp(wlog - m_new)
        decay = jnp.exp(m_prev + ftot[0:1, :] - m_new)
        kw = k * wk
        c_scr[h] = decay * cmat + lax.dot_general(kw.astype(BF16), vb, (((0,), (0,)), ((), ())),
                                                  preferred_element_type=F32)
        n_scr[h:h + 1, :] = decay * nvec + jnp.sum(kw, axis=0, keepdims=True)
        m_scr[h:h + 1, :] = jnp.broadcast_to(m_new, (1, m_scr.shape[1]))

    @pl.when(step == pl.num_programs(1) - 1)
    def _():
        c_ref[0] = c_scr[...]
        n_ref[0] = n_scr[...]
        m_ref[0] = m_scr[0:MLSTM_HEADS, 0:1]


def _mlstm_prompt(z, zg, bias, g_m, batch, seq, q_blk, layer):
    m = z.shape[0]
    wb = g_m.shape[2]
    qw = wb // 2
    dk = qw // MLSTM_HEADS
    dv = wb // MLSTM_HEADS
    L = _pick(seq, (256, 128, 64, 32, 16, 8))
    nc = seq // L
    row = lambda b, c: b * nc + c
    zspec = lambda blk: pl.BlockSpec((L, qw), lambda b, c: (row(b, c), blk))
    est = 2 * 6 * L * qw * 4 + 2 * L * GATE_LANES * 4 + 2 * L * wb * 2 + 3 * MLSTM_HEADS * dk * dv * 4 + 16 * L * L * 4
    return pl.pallas_call(
        _mlstm_prompt_kernel,
        grid=(batch, nc),
        in_specs=[zspec(q_blk + i) for i in range(6)] + [
            pl.BlockSpec((L, GATE_LANES), lambda b, c: (row(b, c), 0)),
            _layer_spec(bias, layer),
            _layer_spec(g_m, layer),
        ],
        out_specs=[
            pl.BlockSpec((L, wb), lambda b, c: (row(b, c), 0)),
            pl.BlockSpec((1, MLSTM_HEADS, dk, dv), lambda b, c: (b, 0, 0, 0)),
            pl.BlockSpec((1, MLSTM_HEADS, dk), lambda b, c: (b, 0, 0)),
            pl.BlockSpec((1, MLSTM_HEADS, 1), lambda b, c: (b, 0, 0)),
        ],
        out_shape=[
            jax.ShapeDtypeStruct((m, wb), BF16),
            jax.ShapeDtypeStruct((batch, MLSTM_HEADS, dk, dv), F32),
            jax.ShapeDtypeStruct((batch, MLSTM_HEADS, dk), F32),
            jax.ShapeDtypeStruct((batch, MLSTM_HEADS, 1), F32),
        ],
        scratch_shapes=[
            pltpu.VMEM((MLSTM_HEADS, dk, dv), F32),
            pltpu.VMEM((MLSTM_HEADS, dk), F32),
            pltpu.VMEM((8, 128), F32),
        ],
        compiler_params=_params(("parallel", "arbitrary"), est),
    )(z, z, z, z, z, z, zg, bias, g_m)


def _mlstm_sample_kernel(q_ref, k_ref, v0_ref, v1_ref, o0_ref, o1_ref, zg_ref, bias_ref, gm_ref,
                         c0_ref, nrow_ref, mrow_ref, n0_ref, *rest, seq_rows):
    yb_ref, c_ref, n_ref, m_ref = rest[-4:]
    R = q_ref.shape[0]
    nb = R // seq_rows
    dk = q_ref.shape[1] // MLSTM_HEADS
    dv = 2 * v0_ref.shape[1] // MLSTM_HEADS
    ri = lax.broadcasted_iota(jnp.int32, (R, R), 0)
    ci = lax.broadcasted_iota(jnp.int32, (R, R), 1)
    same = _seq_of(ri, seq_rows) == _seq_of(ci, seq_rows)
    causal = same & (ci <= ri)
    eye = ci == ri
    rid = lax.broadcasted_iota(jnp.int32, (R, 1), 0)
    for h in range(MLSTM_HEADS):
        q = q_ref[:, h * dk:(h + 1) * dk] * (dk ** -0.5)
        k = k_ref[:, h * dk:(h + 1) * dk]
        v = _head_blocks((v0_ref, v1_ref), h, dv)
        o = _head_blocks((o0_ref, o1_ref), h, dv)
        i_col, i_row, fcum_col, fcum_row, ftot = _mlstm_gates(zg_ref, bias_ref, h, same, causal, eye)
        m_prev = mrow_ref[:, h:h + 1]
        dlog = jnp.where(causal, fcum_col - fcum_row + i_row, -jnp.inf)
        m_inter = m_prev + fcum_col
        m_tok = jnp.maximum(m_inter, jnp.max(dlog, axis=1, keepdims=True))
        qb = q.astype(BF16)
        kb = k.astype(BF16)
        vb = v.astype(BF16)
        s = _dot_nt(qb, kb) * jnp.exp(dlog - m_tok)
        w_inter = jnp.exp(m_inter - m_tok)
        inter = jnp.zeros((R, dv), F32)
        for b in range(nb):
            mine = _seq_of(rid, seq_rows) == b
            inter = inter + jnp.where(mine, _dot(qb, c0_ref[b, h].astype(BF16)), 0.0)
        num = _dot(s.astype(BF16), vb) + w_inter * inter
        nrow = nrow_ref[:, h * dk:(h + 1) * dk]
        den = jnp.sum(s, axis=1, keepdims=True) + w_inter * jnp.sum(q * nrow, axis=1, keepdims=True)
        den = jnp.maximum(jnp.abs(den), jnp.exp(-m_tok))
        hh = num / den
        ms = jnp.mean(hh * hh, axis=-1, keepdims=True)
        yn = (hh * lax.rsqrt(ms + EPS)) * gm_ref[:, h * dv:(h + 1) * dv]
        yb_ref[:, h * dv:(h + 1) * dv] = (yn * jax.nn.sigmoid(o)).astype(yb_ref.dtype)
        wlog_col = ftot - fcum_col + i_col
        wlog_row = _col_to_row(wlog_col, eye)
        m_new = jnp.maximum(m_prev + ftot, jnp.max(jnp.where(same, wlog_row, -jnp.inf), axis=1, keepdims=True))
        wk = jnp.exp(wlog_col - m_new)
        decay = jnp.exp(m_prev + ftot - m_new)
        kw = k * wk
        for b in range(nb):
            mine = _seq_of(rid, seq_rows) == b
            kw_b = jnp.where(mine, kw, 0.0)
            r0 = b * seq_rows
            dec_b = decay[r0:r0 + 1, :]
            upd = lax.dot_general(kw_b.astype(BF16), vb, (((0,), (0,)), ((), ())), preferred_element_type=F32)
            c_ref[b, h] = dec_b * c0_ref[b, h] + upd
            n_ref[b, h:h + 1, :] = dec_b * n0_ref[b, h:h + 1, :] + jnp.sum(kw_b, axis=0, keepdims=True)
            m_ref[0, b:b + 1, h:h + 1] = m_new[r0:r0 + 1, :]


def _mlstm_sample(z, zg, bias, g_m, c0, n0, n_rows, m_rows, c_acc, seq_rows, q_blk, layer):
    m = z.shape[0]
    depth, batch = c0.shape[:2]
    wb = g_m.shape[2]
    qw = wb // 2
    dk = qw // MLSTM_HEADS
    dv = wb // MLSTM_HEADS
    nb = 4
    R = nb * seq_rows
    zspec = lambda blk: pl.BlockSpec((R, qw), lambda s: (s, blk))
    cblk = nb * MLSTM_HEADS * dk * dv * 4
    est = 4 * cblk + 2 * 7 * R * qw * 4 + 2 * R * GATE_LANES * 4 + 2 * R * wb * 2
    in_specs = [zspec(q_blk + i) for i in range(6)] + [
        pl.BlockSpec((R, GATE_LANES), lambda s: (s, 0)),
        _layer_spec(bias, layer),
        _layer_spec(g_m, layer),
        pl.BlockSpec((None, nb, MLSTM_HEADS, dk, dv), lambda s: (layer, s, 0, 0, 0)),
        pl.BlockSpec((None, R, qw), lambda s: (layer, s, 0)),
        pl.BlockSpec((None, R, MLSTM_HEADS), lambda s: (layer, s, 0)),
        pl.BlockSpec((None, nb, MLSTM_HEADS, dk), lambda s: (layer, s, 0, 0)),
    ]
    args = [z, z, z, z, z, z, zg, bias, g_m, c0, n_rows, m_rows, n0]
    aliases = {}
    if c_acc is not None:
        in_specs.append(pl.BlockSpec(memory_space=pl.ANY))
        aliases = {len(args): 1}
        args.append(c_acc)
    yb, c_all, n1, m1 = pl.pallas_call(
        functools.partial(_mlstm_sample_kernel, seq_rows=seq_rows),
        grid=(m // R,),
        in_specs=in_specs,
        out_specs=[
            pl.BlockSpec((R, wb), lambda s: (s, 0)),
            pl.BlockSpec((None, nb, MLSTM_HEADS, dk, dv), lambda s: (layer, s, 0, 0, 0)),
            pl.BlockSpec((nb, MLSTM_HEADS, dk), lambda s: (s, 0, 0)),
            pl.BlockSpec((1, nb, MLSTM_HEADS), lambda s: (s, 0, 0)),
        ],
        out_shape=[
            jax.ShapeDtypeStruct((m, wb), BF16),
            jax.ShapeDtypeStruct(c0.shape, F32),
            jax.ShapeDtypeStruct(n0.shape[1:], F32),
            jax.ShapeDtypeStruct((batch // nb, nb, MLSTM_HEADS), F32),
        ],
        input_output_aliases=aliases,
        compiler_params=_params(("parallel",), est),
    )(*args)
    return yb, c_all, n1, m1.reshape(batch, MLSTM_HEADS)


def kernel(x_prompt, x_sample, state_pool, state_mlstm_c, state_mlstm_n, state_mlstm_m, p_prompt, p_sample, g_mix, w_in, b_igate, b_fgate, w_pool, pool_scale, g_mlstm, w_s, b_s, g_v, w_out, g_ffn, w_ffn_gate, w_ffn_up, w_ffn_down, g_ple, w_ple_gate, w_ple_proj, g_final):
    batch, seq, d = x_prompt.shape
    dec_batch, dec_seq, _ = x_sample.shape
    depth = w_in.shape[0]
    wa = pool_scale.shape[1]
    wb = g_mlstm.shape[1]
    wc = g_v.shape[1]
    n_gates = 2 * MLSTM_HEADS
    off_gate = wa + 2 * (wb // 2) + 2 * wb
    off_c = off_gate + n_gates
    q_blk = wa // (wb // 2)
    u_blk = off_gate // wc

    w_in_t = jnp.swapaxes(w_in, 1, 2)
    wt_gate = jnp.pad(w_in_t[:, off_gate:off_c], ((0, 0), (0, GATE_LANES - n_gates), (0, 0)))
    gate_bias = jnp.pad(jnp.concatenate([b_igate, b_fgate], axis=1), ((0, 0), (0, GATE_LANES - n_gates)))[:, None, :]
    w_pool_b = w_pool.astype(BF16)
    w_fd = w_ffn_down.astype(BF16)
    reps = CHUNK // dec_seq
    ws_sample = jnp.tile(w_s[:, :, :dec_seq, :dec_seq], (1, 1, reps, reps))
    bias_prompt = jnp.swapaxes(b_s, 1, 2)
    bias_sample = jnp.tile(jnp.swapaxes(b_s[:, :, :dec_seq], 1, 2), (1, reps, 1))
    vec = lambda a: a[:, None, :]
    g_m3, g_v3, scale3, g_mix3, g_ffn3, g_ple3 = map(vec, (g_mlstm, g_v, pool_scale, g_mix, g_ffn, g_ple))
    g_fin3 = g_final[None, None, :]

    def next_mix(i):
        return (g_mix3, i + 1) if i + 1 < depth else (g_fin3, 0)

    pp = p_prompt.reshape(depth, batch * seq, -1)
    ps = p_sample.reshape(depth, dec_batch * dec_seq, -1)
    hist_tm = jnp.swapaxes(state_pool, 1, 2)
    n_rows = jnp.repeat(state_mlstm_n.reshape(depth, dec_batch, -1), dec_seq, axis=1)
    m_rows = jnp.repeat(state_mlstm_m, dec_seq, axis=1)

    xp = x_prompt.reshape(batch * seq, d)
    xs = x_sample.reshape(dec_batch * dec_seq, d)
    xpb, ssp = _prep_rows(xp, g_mix3, 0)
    xsb, sss = _prep_rows(xs, g_mix3, 0)
    v_start = ((seq - 1) // CHUNK) * CHUNK
    outs = {k: [] for k in ("pool_p", "pool_s", "c_p", "n_p", "n_s", "m_p", "m_s", "v_p", "v_s")}
    c_s = None

    for i in range(depth):
        zp, zgp = _inproj(xpb, ssp, w_in_t, wt_gate, off_gate, n_gates, i)
        ya = _pool_prompt(zp, seq, w_pool_b, scale3, i)
        yb, c1, n1, m1 = _mlstm_prompt(zp, zgp, gate_bias, g_m3, batch, seq, q_blk, i)
        yc, vn = _chunk_mlp(zp, u_blk, w_s, bias_prompt, g_v3, CHUNK, i)
        xp, xpb, ssp = _wout(ya, yb, yc, w_out, xp, i, g_ffn3, i)
        xp, xpb, ssp = _ffn_down(_ffn_up(xpb, ssp, w_ffn_gate, w_ffn_up, i), w_fd, xp, i, g_ple3, i)
        xp, xpb, ssp = _ple(xpb, ssp, w_ple_gate, pp, w_ple_proj, xp, i, *next_mix(i))
        outs["pool_p"].append(zp.reshape(batch, seq, -1)[:, seq - POOL_BUF:, :wa])
        outs["c_p"].append(c1)
        outs["n_p"].append(n1)
        outs["m_p"].append(m1.reshape(batch, MLSTM_HEADS))
        outs["v_p"].append(vn.reshape(batch, seq, wc)[:, v_start:])
        zs, zgs = _inproj(xsb, sss, w_in_t, wt_gate, off_gate, n_gates, i)
        new_tm = jnp.swapaxes(zs[:, :wa].reshape(dec_batch, dec_seq, wa), 0, 1)
        ya_tm = _pool_sample(hist_tm, new_tm, w_pool_b, scale3, PAST_LEN, i)
        ya = jnp.swapaxes(ya_tm, 0, 1).reshape(dec_batch * dec_seq, wa)
        yb, c_s, n1, m1 = _mlstm_sample(zs, zgs, gate_bias, g_m3, state_mlstm_c, state_mlstm_n, n_rows, m_rows,
                                        c_s, dec_seq, q_blk, i)
        yc, vn = _chunk_mlp(zs, u_blk, ws_sample, bias_sample, g_v3, dec_seq, i)
        xs, xsb, sss = _wout(ya, yb, yc, w_out, xs, i, g_ffn3, i)
        xs, xsb, sss = _ffn_down(_ffn_up(xsb, sss, w_ffn_gate, w_ffn_up, i), w_fd, xs, i, g_ple3, i)
        xs, xsb, sss = _ple(xsb, sss, w_ple_gate, ps, w_ple_proj, xs, i, *next_mix(i))
        outs["pool_s"].append(jnp.concatenate([hist_tm[i], new_tm], axis=0)[dec_seq:])
        outs["n_s"].append(n1)
        outs["m_s"].append(m1)
        outs["v_s"].append(vn.reshape(dec_batch, dec_seq, wc))

    y_prompt = _final_norm(xp, g_final[None, :]).reshape(batch, seq, d)
    y_sample = _final_norm(xs, g_final[None, :]).reshape(dec_batch, dec_seq, d)
    st = lambda k: jnp.stack(outs[k])
    pool_sample = jnp.swapaxes(st("pool_s"), 1, 2)
    return (y_prompt, y_sample, st("pool_p"), pool_sample, st("c_p"), c_s, st("n_p"), st("n_s"),
            st("m_p"), st("m_s"), st("v_p"), st("v_s"))
```

```python
import functools
import math

import jax
import jax.numpy as jnp
from jax import lax
from jax.experimental import pallas as pl
from jax.experimental.pallas import tpu as pltpu

F32 = jnp.float32
BF16 = jnp.bfloat16

EPS = 1e-6
PAST_LEN = 16384
POOL_WINDOWS = (2, 4, 8, 16)
POOL_BUF = max(POOL_WINDOWS) - 1
POOL_HALO = 16
N_POOL = len(POOL_WINDOWS)
MLSTM_HEADS = 4
N_CM = 4
CHUNK = 128
GATE_LANES = 128
SS_LANES = 128

VMEM_SLACK_BYTES = 10 * 2**20
VMEM_CAP_BYTES = 58 * 2**20

_ROW_TILES = (1024, 512, 256, 128, 64, 32, 16, 8)


def _params(sem, vmem_estimate):
    limit = min(int(vmem_estimate) + VMEM_SLACK_BYTES, VMEM_CAP_BYTES)
    return pltpu.CompilerParams(dimension_semantics=sem, vmem_limit_bytes=limit)


def _pick(n, candidates):
    for c in candidates:
        if n % c == 0:
            return c
    return n


def _single(shape, index_map):
    return pl.BlockSpec(shape, index_map, pipeline_mode=pl.Buffered(1))


def _layer_spec(arr, layer):
    nd = arr.ndim - 1
    return pl.BlockSpec((None,) + arr.shape[1:], lambda *_: (layer,) + (0,) * nd)


def _dot(a, b):
    return jnp.dot(a, b, preferred_element_type=F32)


def _dot_nt(a, bt):
    return lax.dot_general(a, bt, (((1,), (1,)), ((), ())), preferred_element_type=F32)


def _gelu_tanh(x):
    return 0.5 * x * (1.0 + jnp.tanh(math.sqrt(2.0 / math.pi) * (x + 0.044715 * (x * x * x))))


def _seq_of(idx, seq_rows):
    if seq_rows & (seq_rows - 1) == 0:
        return lax.shift_right_logical(idx, seq_rows.bit_length() - 1)
    return idx // seq_rows


def _log_sigmoid(x):
    return jnp.minimum(x, 0.0) - jnp.log(1.0 + jnp.exp(-jnp.abs(x)))


def _row_scale(ss_ref, width):
    return lax.rsqrt(ss_ref[:, 0:1] * (1.0 / width) + EPS)


def _gain_spec(g, g_idx):
    return pl.BlockSpec((None,) + g.shape[1:], lambda i, j: (g_idx, 0, 0))


def _emit_rows(xn, j, gn_ref, o_ref, xb_ref, ss_ref):
    bn = xn.shape[1]
    o_ref[...] = xn
    xb_ref[...] = (xn * gn_ref[:, pl.ds(pl.multiple_of(j * bn, bn), bn)]).astype(xb_ref.dtype)
    part = jnp.broadcast_to(jnp.sum(xn * xn, axis=-1, keepdims=True), ss_ref.shape)

    @pl.when(j == 0)
    def _():
        ss_ref[...] = part

    @pl.when(j > 0)
    def _():
        ss_ref[...] += part


def _row_outs(m, d, bm, bn):
    specs = [
        pl.BlockSpec((bm, bn), lambda i, j: (i, j)),
        pl.BlockSpec((bm, bn), lambda i, j: (i, j)),
        pl.BlockSpec((bm, SS_LANES), lambda i, j: (i, 0)),
    ]
    shapes = [jax.ShapeDtypeStruct((m, d), F32), jax.ShapeDtypeStruct((m, d), BF16),
              jax.ShapeDtypeStruct((m, SS_LANES), F32)]
    return specs, shapes


def _prep_rows_kernel(x_ref, gn_ref, xb_ref, ss_ref):
    x = x_ref[...]
    xb_ref[...] = (x * gn_ref[...]).astype(xb_ref.dtype)
    ss_ref[...] = jnp.broadcast_to(jnp.sum(x * x, axis=-1, keepdims=True), ss_ref.shape)


def _prep_rows(x, g, g_idx):
    m, d = x.shape
    bm = _pick(m, (256, 128, 64, 32, 16, 8))
    return pl.pallas_call(
        _prep_rows_kernel,
        grid=(m // bm,),
        in_specs=[pl.BlockSpec((bm, d), lambda i: (i, 0)), _layer_spec(g, g_idx)],
        out_specs=[pl.BlockSpec((bm, d), lambda i: (i, 0)), pl.BlockSpec((bm, SS_LANES), lambda i: (i, 0))],
        out_shape=[jax.ShapeDtypeStruct((m, d), BF16), jax.ShapeDtypeStruct((m, SS_LANES), F32)],
        compiler_params=_params(("parallel",), 2 * bm * d * 6),
    )(x, g)


def _inproj_kernel(xb_ref, ss_ref, wt_ref, wgt_ref, z_ref, zg_ref):
    rs = _row_scale(ss_ref, xb_ref.shape[1])

    @pl.when(pl.program_id(1) == 0)
    def _():
        zg_ref[...] = rs * _dot_nt(xb_ref[...], wgt_ref[...].astype(BF16))

    z_ref[...] = rs * _dot_nt(xb_ref[...], wt_ref[0].astype(BF16))


def _inproj(xb, ss, wt, wt_gate, n_lead, n_skip, layer):
    m, d = xb.shape
    n = wt.shape[1] - n_skip
    bm = _pick(m, _ROW_TILES)
    bn = _pick(math.gcd(n_lead, n), (512, 256, 128))
    lead_blocks = n_lead // bn
    assert n_skip % 8 == 0 and n_lead % bn == 0 and n % bn == 0
    est = (2 * bm * d * 2 + 2 * d * bn * 4 + 2 * d * GATE_LANES * 4 + 2 * bm * bn * 4 + 4 * bm * GATE_LANES * 4
           + d * bn * 2)
    return pl.pallas_call(
        _inproj_kernel,
        grid=(m // bm, n // bn),
        in_specs=[
            pl.BlockSpec((bm, d), lambda i, j: (i, 0)),
            pl.BlockSpec((bm, SS_LANES), lambda i, j: (i, 0)),
            pl.BlockSpec((pl.Element(1), pl.Element(bn), pl.Element(d)),
                         lambda i, j: (layer, pl.multiple_of(j * bn + jnp.where(j >= lead_blocks, n_skip, 0), 8), 0)),
            _layer_spec(wt_gate, layer),
        ],
        out_specs=[
            pl.BlockSpec((bm, bn), lambda i, j: (i, j)),
            pl.BlockSpec((bm, GATE_LANES), lambda i, j: (i, 0)),
        ],
        out_shape=[jax.ShapeDtypeStruct((m, n), F32), jax.ShapeDtypeStruct((m, GATE_LANES), F32)],
        compiler_params=_params(("parallel", "arbitrary"), est),
    )(xb, ss, wt, wt_gate)


def _ffn_up_kernel(xb_ref, ss_ref, wg_ref, wu_ref, o_ref):
    rs = _row_scale(ss_ref, xb_ref.shape[1])
    h = xb_ref[...]
    a = rs * _dot(h, wg_ref[...].astype(BF16))
    b = rs * _dot(h, wu_ref[...].astype(BF16))
    o_ref[...] = ((a * jax.nn.sigmoid(a)) * b).astype(o_ref.dtype)


def _ffn_up(xb, ss, w_gate, w_up, layer):
    m, d = xb.shape
    n = w_gate.shape[2]
    bm = _pick(m, _ROW_TILES)
    tb = _pick(n, (256, 128))
    est = 2 * bm * d * 2 + 4 * d * tb * 4 + 2 * d * tb * 2 + 2 * bm * tb * 2 + 2 * bm * SS_LANES * 4
    wspec = pl.BlockSpec((None, d, tb), lambda i, j: (layer, 0, j))
    return pl.pallas_call(
        _ffn_up_kernel,
        grid=(m // bm, n // tb),
        in_specs=[pl.BlockSpec((bm, d), lambda i, j: (i, 0)), pl.BlockSpec((bm, SS_LANES), lambda i, j: (i, 0)),
                  wspec, wspec],
        out_specs=pl.BlockSpec((bm, tb), lambda i, j: (i, j)),
        out_shape=jax.ShapeDtypeStruct((m, n), BF16),
        compiler_params=_params(("parallel", "arbitrary"), est),
    )(xb, ss, w_gate, w_up)


def _ple_kernel(xb_ref, ss_ref, w_ref, p_ref, wpp_ref, x_ref, gn_ref, o_ref, xbo_ref, sso_ref):
    rs = _row_scale(ss_ref, xb_ref.shape[1])
    gate = jax.nn.sigmoid(rs * _dot(xb_ref[...], w_ref[...].astype(BF16)))
    emb = _dot(p_ref[...].astype(BF16), wpp_ref[...].astype(BF16))
    _emit_rows(x_ref[...] + gate * emb, pl.program_id(1), gn_ref, o_ref, xbo_ref, sso_ref)


def _ple(xb, ss, w_pg, p, w_pp, x, layer, g_next, g_idx):
    m, d = x.shape
    pd = p.shape[2]
    bm = _pick(m, _ROW_TILES)
    bn = _pick(d, (512, 256, 128))
    est = (2 * bm * d * 2 + 2 * d * bn * 4 + d * bn * 2 + 2 * bm * pd * 4 + 2 * pd * bn * 4 + 4 * bm * bn * 4
           + 2 * bm * bn * 2 + 4 * bm * SS_LANES * 4)
    out_specs, out_shape = _row_outs(m, d, bm, bn)
    return pl.pallas_call(
        _ple_kernel,
        grid=(m // bm, d // bn),
        in_specs=[
            pl.BlockSpec((bm, d), lambda i, j: (i, 0)),
            pl.BlockSpec((bm, SS_LANES), lambda i, j: (i, 0)),
            pl.BlockSpec((None, d, bn), lambda i, j: (layer, 0, j)),
            pl.BlockSpec((None, bm, pd), lambda i, j: (layer, i, 0)),
            pl.BlockSpec((None, pd, bn), lambda i, j: (layer, 0, j)),
            pl.BlockSpec((bm, bn), lambda i, j: (i, j)),
            _gain_spec(g_next, g_idx),
        ],
        out_specs=out_specs,
        out_shape=out_shape,
        compiler_params=_params(("parallel", "arbitrary"), est),
    )(xb, ss, w_pg, p, w_pp, x, g_next)


def _wout_kernel(ya_ref, yb_ref, yc_ref, w_ref, x_ref, gn_ref, o_ref, xbo_ref, sso_ref):
    wa = ya_ref.shape[1]
    wb = yb_ref.shape[1]
    xn = x_ref[...] + _dot(ya_ref[...].astype(BF16), w_ref[0:wa, :].astype(BF16))
    xn = xn + _dot(yb_ref[...].astype(BF16), w_ref[wa:wa + wb, :].astype(BF16))
    xn = xn + _dot(yc_ref[...].astype(BF16), w_ref[wa + wb:, :].astype(BF16))
    _emit_rows(xn, pl.program_id(1), gn_ref, o_ref, xbo_ref, sso_ref)


def _wout(ya, yb, yc, w, x, layer, g_next, g_idx):
    m, d = x.shape
    wa, wb, wc = ya.shape[1], yb.shape[1], yc.shape[1]
    k = wa + wb + wc
    bm = _pick(m, _ROW_TILES)
    bn = _pick(d, (512, 256, 128))
    est = (2 * bm * wa * ya.dtype.itemsize + 2 * bm * wb * 2 + 2 * bm * wc * 2
           + 2 * k * bn * 4 + k * bn * 2 + 4 * bm * bn * 4 + 2 * bm * bn * 2 + 2 * bm * SS_LANES * 4)
    out_specs, out_shape = _row_outs(m, d, bm, bn)
    return pl.pallas_call(
        _wout_kernel,
        grid=(m // bm, d // bn),
        in_specs=[
            pl.BlockSpec((bm, wa), lambda i, j: (i, 0)),
            pl.BlockSpec((bm, wb), lambda i, j: (i, 0)),
            pl.BlockSpec((bm, wc), lambda i, j: (i, 0)),
            pl.BlockSpec((None, k, bn), lambda i, j: (layer, 0, j)),
            pl.BlockSpec((bm, bn), lambda i, j: (i, j)),
            _gain_spec(g_next, g_idx),
        ],
        out_specs=out_specs,
        out_shape=out_shape,
        compiler_params=_params(("parallel", "arbitrary"), est),
    )(ya, yb, yc, w, x, g_next)


def _ffn_down_kernel(a_ref, w_ref, x_ref, gn_ref, o_ref, xbo_ref, sso_ref):
    _emit_rows(x_ref[...] + _dot(a_ref[...], w_ref[...]), pl.program_id(1), gn_ref, o_ref, xbo_ref, sso_ref)


def _ffn_down(a, w, x, layer, g_next, g_idx):
    m, d = x.shape
    k = a.shape[1]
    bm = _pick(m, _ROW_TILES[1:])
    bn = _pick(d, (512, 256, 128))
    est = 2 * bm * k * 2 + 2 * k * bn * 2 + 4 * bm * bn * 4 + 2 * bm * bn * 2 + 2 * bm * SS_LANES * 4
    out_specs, out_shape = _row_outs(m, d, bm, bn)
    return pl.pallas_call(
        _ffn_down_kernel,
        grid=(m // bm, d // bn),
        in_specs=[
            pl.BlockSpec((bm, k), lambda i, j: (i, 0)),
            pl.BlockSpec((None, k, bn), lambda i, j: (layer, 0, j)),
            pl.BlockSpec((bm, bn), lambda i, j: (i, j)),
            _gain_spec(g_next, g_idx),
        ],
        out_specs=out_specs,
        out_shape=out_shape,
        compiler_params=_params(("parallel", "arbitrary"), est),
    )(a, w, x, g_next)


def _final_norm_kernel(x_ref, g_ref, o_ref):
    x = x_ref[...]
    ms = jnp.mean(x * x, axis=-1, keepdims=True)
    o_ref[...] = (x * lax.rsqrt(ms + EPS)) * g_ref[...]


def _final_norm(x, g):
    m, d = x.shape
    bm = _pick(m, (256, 128, 64, 32, 16, 8))
    return pl.pallas_call(
        _final_norm_kernel,
        grid=(m // bm,),
        in_specs=[pl.BlockSpec((bm, d), lambda i: (i, 0)), pl.BlockSpec((1, d), lambda i: (0, 0))],
        out_specs=pl.BlockSpec((bm, d), lambda i: (i, 0)),
        out_shape=jax.ShapeDtypeStruct((m, d), F32),
        compiler_params=_params(("parallel",), 4 * bm * d * 4),
    )(x, g)


def _pool_prompt_kernel(a_ref, halo_ref, wp_ref, sc_ref, y_ref, ext_scr, *, tiles_per_seq):
    tp = a_ref.shape[0]
    cg = wp_ref.shape[1]
    tile = pl.program_id(0) % tiles_per_seq
    ext_scr[0:POOL_HALO, :] = jnp.where(tile == 0, 0.0, halo_ref[...])
    ext_scr[POOL_HALO:, :] = a_ref[...]
    pos = tile * tp + lax.broadcasted_iota(jnp.int32, (tp, 1), 0)
    for g, w in enumerate(POOL_WINDOWS):
        sl = slice(g * cg, (g + 1) * cg)
        s = ext_scr[POOL_HALO:POOL_HALO + tp, sl]
        for k in range(1, w):
            s = s + ext_scr[POOL_HALO - k:POOL_HALO - k + tp, sl]
        cnt = jnp.minimum(pos + 1, w).astype(F32)
        dlt = s / cnt - a_ref[:, sl]
        y = _dot(dlt.astype(BF16), wp_ref[g]) * sc_ref[:, sl]
        y_ref[:, sl] = y.astype(y_ref.dtype)


def _pool_prompt(z, seq, w_pool, scale, layer):
    m = z.shape[0]
    wa = scale.shape[2]
    tp = _pick(seq, (512, 256, 128, 64, 32, 16))
    tiles_per_seq = seq // tp
    halo_blocks = tp // POOL_HALO
    est = (2 * tp * wa * 4 + 2 * POOL_HALO * wa * 4 + (tp + POOL_HALO) * wa * 4 + 2 * tp * wa * 2
           + 4 * wa * wa // N_POOL)
    return pl.pallas_call(
        functools.partial(_pool_prompt_kernel, tiles_per_seq=tiles_per_seq),
        grid=(m // tp,),
        in_specs=[
            pl.BlockSpec((tp, wa), lambda i: (i, 0)),
            pl.BlockSpec((POOL_HALO, wa), lambda i: (jnp.maximum(i * halo_blocks - 1, 0), 0)),
            _layer_spec(w_pool, layer),
            _layer_spec(scale, layer),
        ],
        out_specs=pl.BlockSpec((tp, wa), lambda i: (i, 0)),
        out_shape=jax.ShapeDtypeStruct((m, wa), BF16),
        scratch_shapes=[pltpu.VMEM((tp + POOL_HALO, wa), F32)],
        compiler_params=_params(("parallel",), est),
    )(z, z, w_pool, scale)


def _pool_sample_kernel(hist_ref, new_ref, wp_ref, sc_ref, y_ref, *, first_pos):
    cg = wp_ref.shape[1]
    n_new = new_ref.shape[0]
    rows = [hist_ref[r] for r in range(POOL_BUF)] + [new_ref[t] for t in range(n_new)]
    for t in range(n_new):
        for g, w in enumerate(POOL_WINDOWS):
            sl = slice(g * cg, (g + 1) * cg)
            s = rows[POOL_BUF + t][:, sl]
            for k in range(1, w):
                s = s + rows[POOL_BUF + t - k][:, sl]
            cnt = float(min(first_pos + t + 1, w))
            dlt = s / cnt - rows[POOL_BUF + t][:, sl]
            y = _dot(dlt.astype(BF16), wp_ref[g]) * sc_ref[:, sl]
            y_ref[t, :, sl] = y.astype(y_ref.dtype)


def _pool_sample(hist, new, w_pool, scale, first_pos, layer):
    n_new, b, wa = new.shape
    bb = _pick(b, (32, 16, 8))
    est = 2 * (POOL_BUF + 2 * n_new) * bb * wa * 4 + 4 * wa * wa // N_POOL
    return pl.pallas_call(
        functools.partial(_pool_sample_kernel, first_pos=first_pos),
        grid=(b // bb,),
        in_specs=[
            pl.BlockSpec((None, POOL_BUF, bb, wa), lambda i: (layer, 0, i, 0)),
            pl.BlockSpec((n_new, bb, wa), lambda i: (0, i, 0)),
            _layer_spec(w_pool, layer),
            _layer_spec(scale, layer),
        ],
        out_specs=pl.BlockSpec((n_new, bb, wa), lambda i: (0, i, 0)),
        out_shape=jax.ShapeDtypeStruct((n_new, b, wa), F32),
        compiler_params=_params(("parallel",), est),
    )(hist, new, w_pool, scale)


def _chunk_mlp_kernel(u_ref, v_ref, ws_ref, b_ref, gv_ref, yc_ref, vn_ref, *, seq_rows):
    rows = u_ref.shape[0]
    cg = u_ref.shape[1] // N_CM
    ri = lax.broadcasted_iota(jnp.int32, (rows, rows), 0)
    ci = lax.broadcasted_iota(jnp.int32, (rows, rows), 1)
    mask = ci <= ri
    if seq_rows < rows:
        mask = mask & (_seq_of(ri, seq_rows) == _seq_of(ci, seq_rows))
    for g in range(N_CM):
        sl = slice(g * cg, (g + 1) * cg)
        u = _gelu_tanh(u_ref[:, sl])
        v = _gelu_tanh(v_ref[:, sl])
        ms = jnp.mean(v * v, axis=-1, keepdims=True)
        vn = (v * lax.rsqrt(ms + EPS)) * gv_ref[:, sl]
        vn_ref[:, sl] = vn
        wm = jnp.where(mask, ws_ref[g], 0.0).astype(BF16)
        mix = _dot(wm, vn.astype(BF16)) + b_ref[:, g:g + 1]
        yc_ref[:, sl] = (u * mix).astype(yc_ref.dtype)


def _chunk_mlp(z, u_blk, w_mix, bias, g_v, seq_rows, layer):
    m = z.shape[0]
    wc = g_v.shape[2]
    tiles_per_seq = max(seq_rows // CHUNK, 1)
    seq_rows = min(seq_rows, CHUNK)
    assert m % (tiles_per_seq * CHUNK) == 0
    est = 4 * CHUNK * wc * 4 + 2 * N_CM * CHUNK * CHUNK * 4 + 2 * CHUNK * wc * 2 + 2 * CHUNK * wc * 4
    return pl.pallas_call(
        functools.partial(_chunk_mlp_kernel, seq_rows=seq_rows),
        grid=(m // CHUNK,),
        in_specs=[
            pl.BlockSpec((CHUNK, wc), lambda i: (i, u_blk)),
            pl.BlockSpec((CHUNK, wc), lambda i: (i, u_blk + 1)),
            _layer_spec(w_mix, layer),
            _layer_spec(bias, layer),
            _layer_spec(g_v, layer),
        ],
        out_specs=[pl.BlockSpec((CHUNK, wc), lambda i: (i, 0)),
                   pl.BlockSpec((CHUNK, wc), lambda i: (i // tiles_per_seq, 0))],
        out_shape=[jax.ShapeDtypeStruct((m, wc), BF16), jax.ShapeDtypeStruct((m // tiles_per_seq, wc), F32)],
        compiler_params=_params(("arbitrary",), est),
    )(z, z, w_mix, bias, g_v)


def _col_to_row(col, eye):
    return jnp.sum(jnp.where(eye, col, 0.0), axis=0, keepdims=True)


def _head_blocks(refs, h, width):
    r = refs[h // 2]
    return r[:, (h % 2) * width:(h % 2 + 1) * width]


def _mlstm_gates(zg_ref, bias_ref, h, same, causal, eye):
    i_col = zg_ref[:, h:h + 1] + bias_ref[:, h:h + 1]
    f_col = zg_ref[:, MLSTM_HEADS + h:MLSTM_HEADS + h + 1] + bias_ref[:, MLSTM_HEADS + h:MLSTM_HEADS + h + 1]
    lf_col = _log_sigmoid(f_col)
    lf_row = _col_to_row(lf_col, eye)
    i_row = _col_to_row(i_col, eye)
    fcum_col = jnp.sum(jnp.where(causal, lf_row, 0.0), axis=1, keepdims=True)
    fcum_row = _col_to_row(fcum_col, eye)
    ftot_col = jnp.sum(jnp.where(same, lf_row, 0.0), axis=1, keepdims=True)
    return i_col, i_row, fcum_col, fcum_row, ftot_col


def _mlstm_prompt_kernel(q_ref, k_ref, v0_ref, v1_ref, o0_ref, o1_ref, zg_ref, bias_ref, gm_ref,
                         yb_ref, c_ref, n_ref, m_ref, c_scr, n_scr, m_scr):
    L = q_ref.shape[0]
    dk = q_ref.shape[1] // MLSTM_HEADS
    dv = 2 * v0_ref.shape[1] // MLSTM_HEADS
    step = pl.program_id(1)

    @pl.when(step == 0)
    def _():
        c_scr[...] = jnp.zeros_like(c_scr)
        n_scr[...] = jnp.zeros_like(n_scr)
        m_scr[...] = jnp.zeros_like(m_scr)

    ri = lax.broadcasted_iota(jnp.int32, (L, L), 0)
    ci = lax.broadcasted_iota(jnp.int32, (L, L), 1)
    causal = ci <= ri
    eye = ci == ri
    same = ci >= 0
    for h in range(MLSTM_HEADS):
        q = q_ref[:, h * dk:(h + 1) * dk] * (dk ** -0.5)
        k = k_ref[:, h * dk:(h + 1) * dk]
        v = _head_blocks((v0_ref, v1_ref), h, dv)
        o = _head_blocks((o0_ref, o1_ref), h, dv)
        i_col, i_row, fcum_col, fcum_row, ftot = _mlstm_gates(zg_ref, bias_ref, h, same, causal, eye)
        m_prev = m_scr[h:h + 1, 0:1]
        dlog = jnp.where(causal, fcum_col - fcum_row + i_row, -jnp.inf)
        m_inter = m_prev + fcum_col
        m_tok = jnp.maximum(m_inter, jnp.max(dlog, axis=1, keepdims=True))
        qb = q.astype(BF16)
        kb = k.astype(BF16)
        vb = v.astype(BF16)
        s = _dot_nt(qb, kb) * jnp.exp(dlog - m_tok)
        w_inter = jnp.exp(m_inter - m_tok)
        cmat = c_scr[h]
        nvec = n_scr[h:h + 1, :]
        num = _dot(s.astype(BF16), vb) + w_inter * _dot(qb, cmat.astype(BF16))
        den = jnp.sum(s, axis=1, keepdims=True) + w_inter * jnp.sum(q * nvec, axis=1, keepdims=True)
        den = jnp.maximum(jnp.abs(den), jnp.exp(-m_tok))
        hh = num / den
        ms = jnp.mean(hh * hh, axis=-1, keepdims=True)
        yn = (hh * lax.rsqrt(ms + EPS)) * gm_ref[:, h * dv:(h + 1) * dv]
        yb_ref[:, h * dv:(h + 1) * dv] = (yn * jax.nn.sigmoid(o)).astype(yb_ref.dtype)
        wlog = ftot - fcum_col + i_col
        m_new = jnp.maximum(m_prev + ftot[0:1, :], jnp.max(wlog, axis=0, keepdims=True))
        wk = jnp.exp(wlog - m_new)
        decay = jnp.exp(m_prev + ftot[0:1, :] - m_new)
        kw = k * wk
        c_scr[h] = decay * cmat + lax.dot_general(kw.astype(BF16), vb, (((0,), (0,)), ((), ())),
                                                  preferred_element_type=F32)
        n_scr[h:h + 1, :] = decay * nvec + jnp.sum(kw, axis=0, keepdims=True)
        m_scr[h:h + 1, :] = jnp.broadcast_to(m_new, (1, m_scr.shape[1]))

    @pl.when(step == pl.num_programs(1) - 1)
    def _():
        c_ref[0] = c_scr[...]
        n_ref[0] = n_scr[...]
        m_ref[0] = m_scr[0:MLSTM_HEADS, 0:1]


def _mlstm_prompt(z, zg, bias, g_m, batch, seq, q_blk, layer):
    m = z.shape[0]
    wb = g_m.shape[2]
    qw = wb // 2
    dk = qw // MLSTM_HEADS
    dv = wb // MLSTM_HEADS
    L = _pick(seq, (256, 128, 64, 32, 16, 8))
    nc = seq // L
    row = lambda b, c: b * nc + c
    zspec = lambda blk: pl.BlockSpec((L, qw), lambda b, c: (row(b, c), blk))
    est = 2 * 6 * L * qw * 4 + 2 * L * GATE_LANES * 4 + 2 * L * wb * 2 + 3 * MLSTM_HEADS * dk * dv * 4 + 16 * L * L * 4
    return pl.pallas_call(
        _mlstm_prompt_kernel,
        grid=(batch, nc),
        in_specs=[zspec(q_blk + i) for i in range(6)] + [
            pl.BlockSpec((L, GATE_LANES), lambda b, c: (row(b, c), 0)),
            _layer_spec(bias, layer),
            _layer_spec(g_m, layer),
        ],
        out_specs=[
            pl.BlockSpec((L, wb), lambda b, c: (row(b, c), 0)),
            pl.BlockSpec((1, MLSTM_HEADS, dk, dv), lambda b, c: (b, 0, 0, 0)),
            pl.BlockSpec((1, MLSTM_HEADS, dk), lambda b, c: (b, 0, 0)),
            pl.BlockSpec((1, MLSTM_HEADS, 1), lambda b, c: (b, 0, 0)),
        ],
        out_shape=[
            jax.ShapeDtypeStruct((m, wb), BF16),
            jax.ShapeDtypeStruct((batch, MLSTM_HEADS, dk, dv), F32),
            jax.ShapeDtypeStruct((batch, MLSTM_HEADS, dk), F32),
            jax.ShapeDtypeStruct((batch, MLSTM_HEADS, 1), F32),
        ],
        scratch_shapes=[
            pltpu.VMEM((MLSTM_HEADS, dk, dv), F32),
            pltpu.VMEM((MLSTM_HEADS, dk), F32),
            pltpu.VMEM((8, 128), F32),
        ],
        compiler_params=_params(("parallel", "arbitrary"), est),
    )(z, z, z, z, z, z, zg, bias, g_m)


def _mlstm_sample_kernel(q_ref, k_ref, v0_ref, v1_ref, o0_ref, o1_ref, zg_ref, bias_ref, gm_ref,
                         c0_ref, nrow_ref, mrow_ref, n0_ref, *rest, seq_rows):
    yb_ref, c_ref, n_ref, m_ref = rest[-4:]
    R = q_ref.shape[0]
    nb = R // seq_rows
    dk = q_ref.shape[1] // MLSTM_HEADS
    dv = 2 * v0_ref.shape[1] // MLSTM_HEADS
    ri = lax.broadcasted_iota(jnp.int32, (R, R), 0)
    ci = lax.broadcasted_iota(jnp.int32, (R, R), 1)
    same = _seq_of(ri, seq_rows) == _seq_of(ci, seq_rows)
    causal = same & (ci <= ri)
    eye = ci == ri
    rid = lax.broadcasted_iota(jnp.int32, (R, 1), 0)
    for h in range(MLSTM_HEADS):
        q = q_ref[:, h * dk:(h + 1) * dk] * (dk ** -0.5)
        k = k_ref[:, h * dk:(h + 1) * dk]
        v = _head_blocks((v0_ref, v1_ref), h, dv)
        o = _head_blocks((o0_ref, o1_ref), h, dv)
        i_col, i_row, fcum_col, fcum_row, ftot = _mlstm_gates(zg_ref, bias_ref, h, same, causal, eye)
        m_prev = mrow_ref[:, h:h + 1]
        dlog = jnp.where(causal, fcum_col - fcum_row + i_row, -jnp.inf)
        m_inter = m_prev + fcum_col
        m_tok = jnp.maximum(m_inter, jnp.max(dlog, axis=1, keepdims=True))
        qb = q.astype(BF16)
        kb = k.astype(BF16)
        vb = v.astype(BF16)
        s = _dot_nt(qb, kb) * jnp.exp(dlog - m_tok)
        w_inter = jnp.exp(m_inter - m_tok)
        inter = jnp.zeros((R, dv), F32)
        for b in range(nb):
            mine = _seq_of(rid, seq_rows) == b
            inter = inter + jnp.where(mine, _dot(qb, c0_ref[b, h].astype(BF16)), 0.0)
        num = _dot(s.astype(BF16), vb) + w_inter * inter
        nrow = nrow_ref[:, h * dk:(h + 1) * dk]
        den = jnp.sum(s, axis=1, keepdims=True) + w_inter * jnp.sum(q * nrow, axis=1, keepdims=True)
        den = jnp.maximum(jnp.abs(den), jnp.exp(-m_tok))
        hh = num / den
        ms = jnp.mean(hh * hh, axis=-1, keepdims=True)
        yn = (hh * lax.rsqrt(ms + EPS)) * gm_ref[:, h * dv:(h + 1) * dv]
        yb_ref[:, h * dv:(h + 1) * dv] = (yn * jax.nn.sigmoid(o)).astype(yb_ref.dtype)
        wlog_col = ftot - fcum_col + i_col
        wlog_row = _col_to_row(wlog_col, eye)
        m_new = jnp.maximum(m_prev + ftot, jnp.max(jnp.where(same, wlog_row, -jnp.inf), axis=1, keepdims=True))
        wk = jnp.exp(wlog_col - m_new)
        decay = jnp.exp(m_prev + ftot - m_new)
        kw = k * wk
        for b in range(nb):
            mine = _seq_of(rid, seq_rows) == b
            kw_b = jnp.where(mine, kw, 0.0)
            r0 = b * seq_rows
            dec_b = decay[r0:r0 + 1, :]
            upd = lax.dot_general(kw_b.astype(BF16), vb, (((0,), (0,)), ((), ())), preferred_element_type=F32)
            c_ref[b, h] = dec_b * c0_ref[b, h] + upd
            n_ref[b, h:h + 1, :] = dec_b * n0_ref[b, h:h + 1, :] + jnp.sum(kw_b, axis=0, keepdims=True)
            m_ref[0, b:b + 1, h:h + 1] = m_new[r0:r0 + 1, :]


def _mlstm_sample(z, zg, bias, g_m, c0, n0, n_rows, m_rows, c_acc, seq_rows, q_blk, layer):
    m = z.shape[0]
    depth, batch = c0.shape[:2]
    wb = g_m.shape[2]
    qw = wb // 2
    dk = qw // MLSTM_HEADS
    dv = wb // MLSTM_HEADS
    nb = 4
    R = nb * seq_rows
    zspec = lambda blk: pl.BlockSpec((R, qw), lambda s: (s, blk))
    cblk = nb * MLSTM_HEADS * dk * dv * 4
    est = 4 * cblk + 2 * 7 * R * qw * 4 + 2 * R * GATE_LANES * 4 + 2 * R * wb * 2
    in_specs = [zspec(q_blk + i) for i in range(6)] + [
        pl.BlockSpec((R, GATE_LANES), lambda s: (s, 0)),
        _layer_spec(bias, layer),
        _layer_spec(g_m, layer),
        pl.BlockSpec((None, nb, MLSTM_HEADS, dk, dv), lambda s: (layer, s, 0, 0, 0)),
        pl.BlockSpec((None, R, qw), lambda s: (layer, s, 0)),
        pl.BlockSpec((None, R, MLSTM_HEADS), lambda s: (layer, s, 0)),
        pl.BlockSpec((None, nb, MLSTM_HEADS, dk), lambda s: (layer, s, 0, 0)),
    ]
    args = [z, z, z, z, z, z, zg, bias, g_m, c0, n_rows, m_rows, n0]
    aliases = {}
    if c_acc is not None:
        in_specs.append(pl.BlockSpec(memory_space=pl.ANY))
        aliases = {len(args): 1}
        args.append(c_acc)
    yb, c_all, n1, m1 = pl.pallas_call(
        functools.partial(_mlstm_sample_kernel, seq_rows=seq_rows),
        grid=(m // R,),
        in_specs=in_specs,
        out_specs=[
            pl.BlockSpec((R, wb), lambda s: (s, 0)),
            pl.BlockSpec((None, nb, MLSTM_HEADS, dk, dv), lambda s: (layer, s, 0, 0, 0)),
            pl.BlockSpec((nb, MLSTM_HEADS, dk), lambda s: (s, 0, 0)),
            pl.BlockSpec((1, nb, MLSTM_HEADS), lambda s: (s, 0, 0)),
        ],
        out_shape=[
            jax.ShapeDtypeStruct((m, wb), BF16),
            jax.ShapeDtypeStruct(c0.shape, F32),
            jax.ShapeDtypeStruct(n0.shape[1:], F32),
            jax.ShapeDtypeStruct((batch // nb, nb, MLSTM_HEADS), F32),
        ],
        input_output_aliases=aliases,
        compiler_params=_params(("parallel",), est),
    )(*args)
    return yb, c_all, n1, m1.reshape(batch, MLSTM_HEADS)


def kernel(x_prompt, x_sample, state_pool, state_mlstm_c, state_mlstm_n, state_mlstm_m, p_prompt, p_sample, g_mix, w_in, b_igate, b_fgate, w_pool, pool_scale, g_mlstm, w_s, b_s, g_v, w_out, g_ffn, w_ffn_gate, w_ffn_up, w_ffn_down, g_ple, w_ple_gate, w_ple_proj, g_final):
    batch, seq, d = x_prompt.shape
    dec_batch, dec_seq, _ = x_sample.shape
    depth = w_in.shape[0]
    wa = pool_scale.shape[1]
    wb = g_mlstm.shape[1]
    wc = g_v.shape[1]
    n_gates = 2 * MLSTM_HEADS
    off_gate = wa + 2 * (wb // 2) + 2 * wb
    off_c = off_gate + n_gates
    q_blk = wa // (wb // 2)
    u_blk = off_gate // wc

    w_in_t = jnp.swapaxes(w_in, 1, 2)
    wt_gate = jnp.pad(w_in_t[:, off_gate:off_c], ((0, 0), (0, GATE_LANES - n_gates), (0, 0)))
    gate_bias = jnp.pad(jnp.concatenate([b_igate, b_fgate], axis=1), ((0, 0), (0, GATE_LANES - n_gates)))[:, None, :]
    w_pool_b = w_pool.astype(BF16)
    w_fd = w_ffn_down.astype(BF16)
    reps = CHUNK // dec_seq
    ws_sample = jnp.tile(w_s[:, :, :dec_seq, :dec_seq], (1, 1, reps, reps))
    bias_prompt = jnp.swapaxes(b_s, 1, 2)
    bias_sample = jnp.tile(jnp.swapaxes(b_s[:, :, :dec_seq], 1, 2), (1, reps, 1))
    vec = lambda a: a[:, None, :]
    g_m3, g_v3, scale3, g_mix3, g_ffn3, g_ple3 = map(vec, (g_mlstm, g_v, pool_scale, g_mix, g_ffn, g_ple))
    g_fin3 = g_final[None, None, :]

    def next_mix(i):
        return (g_mix3, i + 1) if i + 1 < depth else (g_fin3, 0)

    pp = p_prompt.reshape(depth, batch * seq, -1)
    ps = p_sample.reshape(depth, dec_batch * dec_seq, -1)
    hist_tm = jnp.swapaxes(state_pool, 1, 2)
    n_rows = jnp.repeat(state_mlstm_n.reshape(depth, dec_batch, -1), dec_seq, axis=1)
    m_rows = jnp.repeat(state_mlstm_m, dec_seq, axis=1)

    xp = x_prompt.reshape(batch * seq, d)
    xs = x_sample.reshape(dec_batch * dec_seq, d)
    xpb, ssp = _prep_rows(xp, g_mix3, 0)
    xsb, sss = _prep_rows(xs, g_mix3, 0)
    v_start = ((seq - 1) // CHUNK) * CHUNK
    outs = {k: [] for k in ("pool_p", "pool_s", "c_p", "n_p", "n_s", "m_p", "m_s", "v_p", "v_s")}
    c_s = None

    for i in range(depth):
        zp, zgp = _inproj(xpb, ssp, w_in_t, wt_gate, off_gate, n_gates, i)
        ya = _pool_prompt(zp, seq, w_pool_b, scale3, i)
        yb, c1, n1, m1 = _mlstm_prompt(zp, zgp, gate_bias, g_m3, batch, seq, q_blk, i)
        yc, vn = _chunk_mlp(zp, u_blk, w_s, bias_prompt, g_v3, seq, i)
        xp, xpb, ssp = _wout(ya, yb, yc, w_out, xp, i, g_ffn3, i)
        xp, xpb, ssp = _ffn_down(_ffn_up(xpb, ssp, w_ffn_gate, w_ffn_up, i), w_fd, xp, i, g_ple3, i)
        xp, xpb, ssp = _ple(xpb, ssp, w_ple_gate, pp, w_ple_proj, xp, i, *next_mix(i))
        outs["pool_p"].append(zp.reshape(batch, seq, -1)[:, seq - POOL_BUF:, :wa])
        outs["c_p"].append(c1)
        outs["n_p"].append(n1)
        outs["m_p"].append(m1.reshape(batch, MLSTM_HEADS))
        outs["v_p"].append(vn.reshape(batch, seq - v_start, wc))
        zs, zgs = _inproj(xsb, sss, w_in_t, wt_gate, off_gate, n_gates, i)
        new_tm = jnp.swapaxes(zs[:, :wa].reshape(dec_batch, dec_seq, wa), 0, 1)
        ya_tm = _pool_sample(hist_tm, new_tm, w_pool_b, scale3, PAST_LEN, i)
        ya = jnp.swapaxes(ya_tm, 0, 1).reshape(dec_batch * dec_seq, wa)
        yb, c_s, n1, m1 = _mlstm_sample(zs, zgs, gate_bias, g_m3, state_mlstm_c, state_mlstm_n, n_rows, m_rows,
                                        c_s, dec_seq, q_blk, i)
        yc, vn = _chunk_mlp(zs, u_blk, ws_sample, bias_sample, g_v3, dec_seq, i)
        xs, xsb, sss = _wout(ya, yb, yc, w_out, xs, i, g_ffn3, i)
        xs, xsb, sss = _ffn_down(_ffn_up(xsb, sss, w_ffn_gate, w_ffn_up, i), w_fd, xs, i, g_ple3, i)
        xs, xsb, sss = _ple(xsb, sss, w_ple_gate, ps, w_ple_proj, xs, i, *next_mix(i))
        outs["pool_s"].append(jnp.concatenate([hist_tm[i], new_tm], axis=0)[dec_seq:])
        outs["n_s"].append(n1)
        outs["m_s"].append(m1)
        outs["v_s"].append(vn.reshape(dec_batch, dec_seq, wc))

    y_prompt = _final_norm(xp, g_final[None, :]).reshape(batch, seq, d)
    y_sample = _final_norm(xs, g_final[None, :]).reshape(dec_batch, dec_seq, d)
    st = lambda k: jnp.stack(outs[k])
    pool_sample = jnp.swapaxes(st("pool_s"), 1, 2)
    return (y_prompt, y_sample, st("pool_p"), pool_sample, st("c_p"), c_s, st("n_p"), st("n_s"),
            st("m_p"), st("m_s"), st("v_p"), st("v_s"))
```

```python
import functools
import math

import jax
import jax.numpy as jnp
from jax import lax
from jax.experimental import pallas as pl
from jax.experimental.pallas import tpu as pltpu

F32 = jnp.float32
BF16 = jnp.bfloat16

EPS = 1e-6
PAST_LEN = 16384
POOL_WINDOWS = (2, 4, 8, 16)
POOL_BUF = max(POOL_WINDOWS) - 1
POOL_HALO = 16
N_POOL = len(POOL_WINDOWS)
MLSTM_HEADS = 4
N_CM = 4
CHUNK = 128
GATE_LANES = 128
SS_LANES = 128

VMEM_SLACK_BYTES = 10 * 2**20
VMEM_CAP_BYTES = 58 * 2**20

_ROW_TILES = (1024, 512, 256, 128, 64, 32, 16, 8)


def _params(sem, vmem_estimate):
    limit = min(int(vmem_estimate) + VMEM_SLACK_BYTES, VMEM_CAP_BYTES)
    return pltpu.CompilerParams(dimension_semantics=sem, vmem_limit_bytes=limit)


def _pick(n, candidates):
    for c in candidates:
        if n % c == 0:
            return c
    return n


def _single(shape, index_map):
    return pl.BlockSpec(shape, index_map, pipeline_mode=pl.Buffered(1))


def _layer_spec(arr, layer):
    nd = arr.ndim - 1
    return pl.BlockSpec((None,) + arr.shape[1:], lambda *_: (layer,) + (0,) * nd)


def _dot(a, b):
    return jnp.dot(a, b, preferred_element_type=F32)


def _dot_nt(a, bt):
    return lax.dot_general(a, bt, (((1,), (1,)), ((), ())), preferred_element_type=F32)


def _gelu_tanh(x):
    return 0.5 * x * (1.0 + jnp.tanh(math.sqrt(2.0 / math.pi) * (x + 0.044715 * (x * x * x))))


def _seq_of(idx, seq_rows):
    if seq_rows & (seq_rows - 1) == 0:
        return lax.shift_right_logical(idx, seq_rows.bit_length() - 1)
    return idx // seq_rows


def _log_sigmoid(x):
    return jnp.minimum(x, 0.0) - jnp.log(1.0 + jnp.exp(-jnp.abs(x)))


def _row_scale(ss_ref, width):
    return lax.rsqrt(ss_ref[:, 0:1] * (1.0 / width) + EPS)


def _gain_spec(g, g_idx):
    return pl.BlockSpec((None,) + g.shape[1:], lambda i, j: (g_idx, 0, 0))


MXU_COLS = 256


def _col_groups(bn):
    w = MXU_COLS if bn % MXU_COLS == 0 else bn
    return [slice(c, c + w) for c in range(0, bn, w)]


def _group_specs(n_rows, bn, layer):
    groups = _col_groups(bn)
    gw = bn // len(groups)
    return [pl.BlockSpec((None, n_rows, gw), lambda i, j, g=g: (layer, 0, j * len(groups) + g))
            for g in range(len(groups))]


def _emit_rows(new_cols, j, gn_ref, o_ref, xb_ref, ss_ref):
    bn = o_ref.shape[1]
    sq = None
    for grp, cols in enumerate(_col_groups(bn)):
        xn = new_cols(grp, cols)
        o_ref[:, cols] = xn
        g = gn_ref[:, pl.ds(pl.multiple_of(j * bn + cols.start, cols.stop - cols.start), cols.stop - cols.start)]
        xb_ref[:, cols] = (xn * g).astype(xb_ref.dtype)
        s = jnp.sum(xn * xn, axis=-1, keepdims=True)
        sq = s if sq is None else sq + s
    part = jnp.broadcast_to(sq, ss_ref.shape)

    @pl.when(j == 0)
    def _():
        ss_ref[...] = part

    @pl.when(j > 0)
    def _():
        ss_ref[...] += part


def _row_outs(m, d, bm, bn):
    specs = [
        pl.BlockSpec((bm, bn), lambda i, j: (i, j)),
        pl.BlockSpec((bm, bn), lambda i, j: (i, j)),
        pl.BlockSpec((bm, SS_LANES), lambda i, j: (i, 0)),
    ]
    shapes = [jax.ShapeDtypeStruct((m, d), F32), jax.ShapeDtypeStruct((m, d), BF16),
              jax.ShapeDtypeStruct((m, SS_LANES), F32)]
    return specs, shapes


def _prep_rows_kernel(x_ref, gn_ref, xb_ref, ss_ref):
    x = x_ref[...]
    xb_ref[...] = (x * gn_ref[...]).astype(xb_ref.dtype)
    ss_ref[...] = jnp.broadcast_to(jnp.sum(x * x, axis=-1, keepdims=True), ss_ref.shape)


def _prep_rows(x, g, g_idx):
    m, d = x.shape
    bm = _pick(m, (256, 128, 64, 32, 16, 8))
    return pl.pallas_call(
        _prep_rows_kernel,
        grid=(m // bm,),
        in_specs=[pl.BlockSpec((bm, d), lambda i: (i, 0)), _layer_spec(g, g_idx)],
        out_specs=[pl.BlockSpec((bm, d), lambda i: (i, 0)), pl.BlockSpec((bm, SS_LANES), lambda i: (i, 0))],
        out_shape=[jax.ShapeDtypeStruct((m, d), BF16), jax.ShapeDtypeStruct((m, SS_LANES), F32)],
        compiler_params=_params(("parallel",), 2 * bm * d * 6),
    )(x, g)


def _inproj_kernel(xb_ref, ss_ref, *refs):
    wgt_ref, z_ref, zg_ref = refs[-3:]
    wt_refs = refs[:-3]
    rs = _row_scale(ss_ref, xb_ref.shape[1])

    @pl.when(pl.program_id(1) == 0)
    def _():
        zg_ref[...] = rs * _dot_nt(xb_ref[...], wgt_ref[...].astype(BF16))

    for grp, cols in enumerate(_col_groups(z_ref.shape[1])):
        z_ref[:, cols] = rs * _dot_nt(xb_ref[...], wt_refs[grp][0].astype(BF16))


def _inproj(xb, ss, wt, wt_gate, n_lead, n_skip, layer):
    m, d = xb.shape
    n = wt.shape[1] - n_skip
    bm = _pick(m, _ROW_TILES)
    bn = _pick(math.gcd(n_lead, n), (512, 256, 128))
    lead_blocks = n_lead // bn
    assert n_skip % 8 == 0 and n_lead % bn == 0 and n % bn == 0
    est = (2 * bm * d * 2 + 2 * d * bn * 4 + 2 * d * GATE_LANES * 4 + 2 * bm * bn * 4 + 4 * bm * GATE_LANES * 4
           + d * bn * 2)
    groups = _col_groups(bn)
    gw = bn // len(groups)

    def rows_of(grp):
        def index(i, j):
            row = j * bn + grp * gw + jnp.where(j >= lead_blocks, n_skip, 0)
            return (layer, pl.multiple_of(row, 8), 0)
        return pl.BlockSpec((pl.Element(1), pl.Element(gw), pl.Element(d)), index)

    return pl.pallas_call(
        _inproj_kernel,
        grid=(m // bm, n // bn),
        in_specs=[
            pl.BlockSpec((bm, d), lambda i, j: (i, 0)),
            pl.BlockSpec((bm, SS_LANES), lambda i, j: (i, 0)),
            *[rows_of(grp) for grp in range(len(groups))],
            _layer_spec(wt_gate, layer),
        ],
        out_specs=[
            pl.BlockSpec((bm, bn), lambda i, j: (i, j)),
            pl.BlockSpec((bm, GATE_LANES), lambda i, j: (i, 0)),
        ],
        out_shape=[jax.ShapeDtypeStruct((m, n), F32), jax.ShapeDtypeStruct((m, GATE_LANES), F32)],
        compiler_params=_params(("parallel", "arbitrary"), est),
    )(xb, ss, *[wt] * len(groups), wt_gate)


def _ffn_up_kernel(xb_ref, ss_ref, wg_ref, wu_ref, o_ref):
    rs = _row_scale(ss_ref, xb_ref.shape[1])
    h = xb_ref[...]
    a = rs * _dot(h, wg_ref[...].astype(BF16))
    b = rs * _dot(h, wu_ref[...].astype(BF16))
    o_ref[...] = ((a * jax.nn.sigmoid(a)) * b).astype(o_ref.dtype)


def _ffn_up(xb, ss, w_gate, w_up, layer):
    m, d = xb.shape
    n = w_gate.shape[2]
    bm = _pick(m, _ROW_TILES)
    tb = _pick(n, (256, 128))
    est = 2 * bm * d * 2 + 4 * d * tb * 4 + 2 * d * tb * 2 + 2 * bm * tb * 2 + 2 * bm * SS_LANES * 4
    wspec = pl.BlockSpec((None, d, tb), lambda i, j: (layer, 0, j))
    return pl.pallas_call(
        _ffn_up_kernel,
        grid=(m // bm, n // tb),
        in_specs=[pl.BlockSpec((bm, d), lambda i, j: (i, 0)), pl.BlockSpec((bm, SS_LANES), lambda i, j: (i, 0)),
                  wspec, wspec],
        out_specs=pl.BlockSpec((bm, tb), lambda i, j: (i, j)),
        out_shape=jax.ShapeDtypeStruct((m, n), BF16),
        compiler_params=_params(("parallel", "arbitrary"), est),
    )(xb, ss, w_gate, w_up)


def _ple_kernel(xb_ref, ss_ref, *refs):
    p_ref, wpp_ref, x_ref, gn_ref, o_ref, xbo_ref, sso_ref = refs[-7:]
    w_refs = refs[:-7]
    rs = _row_scale(ss_ref, xb_ref.shape[1])
    pb = p_ref[...].astype(BF16)

    def new_cols(grp, cols):
        gate = jax.nn.sigmoid(rs * _dot(xb_ref[...], w_refs[grp][...].astype(BF16)))
        emb = _dot(pb, wpp_ref[:, cols].astype(BF16))
        return x_ref[:, cols] + gate * emb

    _emit_rows(new_cols, pl.program_id(1), gn_ref, o_ref, xbo_ref, sso_ref)


def _ple(xb, ss, w_pg, p, w_pp, x, layer, g_next, g_idx):
    m, d = x.shape
    pd = p.shape[2]
    bm = _pick(m, _ROW_TILES)
    bn = _pick(d, (512, 256, 128))
    est = (2 * bm * d * 2 + 2 * d * bn * 4 + d * bn * 2 + 2 * bm * pd * 4 + 2 * pd * bn * 4 + 4 * bm * bn * 4
           + 2 * bm * bn * 2 + 4 * bm * SS_LANES * 4)
    out_specs, out_shape = _row_outs(m, d, bm, bn)
    w_specs = _group_specs(d, bn, layer)
    return pl.pallas_call(
        _ple_kernel,
        grid=(m // bm, d // bn),
        in_specs=[
            pl.BlockSpec((bm, d), lambda i, j: (i, 0)),
            pl.BlockSpec((bm, SS_LANES), lambda i, j: (i, 0)),
            *w_specs,
            pl.BlockSpec((None, bm, pd), lambda i, j: (layer, i, 0)),
            pl.BlockSpec((None, pd, bn), lambda i, j: (layer, 0, j)),
            pl.BlockSpec((bm, bn), lambda i, j: (i, j)),
            _gain_spec(g_next, g_idx),
        ],
        out_specs=out_specs,
        out_shape=out_shape,
        compiler_params=_params(("parallel", "arbitrary"), est),
    )(xb, ss, *[w_pg] * len(w_specs), p, w_pp, x, g_next)


def _wout_kernel(ya_ref, yb_ref, yc_ref, *refs):
    x_ref, gn_ref, o_ref, xbo_ref, sso_ref = refs[-5:]
    w_refs = refs[:-5]
    wa = ya_ref.shape[1]
    wb = yb_ref.shape[1]
    ya = ya_ref[...].astype(BF16)

    def new_cols(grp, cols):
        w_ref = w_refs[grp]
        xn = x_ref[:, cols] + _dot(ya, w_ref[0:wa, :].astype(BF16))
        xn = xn + _dot(yb_ref[...], w_ref[wa:wa + wb, :].astype(BF16))
        return xn + _dot(yc_ref[...], w_ref[wa + wb:, :].astype(BF16))

    _emit_rows(new_cols, pl.program_id(1), gn_ref, o_ref, xbo_ref, sso_ref)


def _wout(ya, yb, yc, w, x, layer, g_next, g_idx):
    m, d = x.shape
    wa, wb, wc = ya.shape[1], yb.shape[1], yc.shape[1]
    k = wa + wb + wc
    bm = _pick(m, _ROW_TILES)
    bn = _pick(d, (512, 256, 128))
    est = (2 * bm * wa * ya.dtype.itemsize + 2 * bm * wb * 2 + 2 * bm * wc * 2
           + 2 * k * bn * 4 + k * bn * 2 + 4 * bm * bn * 4 + 2 * bm * bn * 2 + 2 * bm * SS_LANES * 4)
    out_specs, out_shape = _row_outs(m, d, bm, bn)
    w_specs = _group_specs(k, bn, layer)
    return pl.pallas_call(
        _wout_kernel,
        grid=(m // bm, d // bn),
        in_specs=[
            pl.BlockSpec((bm, wa), lambda i, j: (i, 0)),
            pl.BlockSpec((bm, wb), lambda i, j: (i, 0)),
            pl.BlockSpec((bm, wc), lambda i, j: (i, 0)),
            *w_specs,
            pl.BlockSpec((bm, bn), lambda i, j: (i, j)),
            _gain_spec(g_next, g_idx),
        ],
        out_specs=out_specs,
        out_shape=out_shape,
        compiler_params=_params(("parallel", "arbitrary"), est),
    )(ya, yb, yc, *[w] * len(w_specs), x, g_next)


def _ffn_down_kernel(a_ref, *refs):
    x_ref, gn_ref, o_ref, xbo_ref, sso_ref = refs[-5:]
    w_refs = refs[:-5]
    _emit_rows(lambda grp, cols: x_ref[:, cols] + _dot(a_ref[...], w_refs[grp][...]),
               pl.program_id(1), gn_ref, o_ref, xbo_ref, sso_ref)


def _ffn_down(a, w, x, layer, g_next, g_idx):
    m, d = x.shape
    k = a.shape[1]
    bm = _pick(m, _ROW_TILES[1:])
    bn = _pick(d, (512, 256, 128))
    est = 2 * bm * k * 2 + 2 * k * bn * 2 + 4 * bm * bn * 4 + 2 * bm * bn * 2 + 2 * bm * SS_LANES * 4
    out_specs, out_shape = _row_outs(m, d, bm, bn)
    w_specs = _group_specs(k, bn, layer)
    return pl.pallas_call(
        _ffn_down_kernel,
        grid=(m // bm, d // bn),
        in_specs=[
            pl.BlockSpec((bm, k), lambda i, j: (i, 0)),
            *w_specs,
            pl.BlockSpec((bm, bn), lambda i, j: (i, j)),
            _gain_spec(g_next, g_idx),
        ],
        out_specs=out_specs,
        out_shape=out_shape,
        compiler_params=_params(("parallel", "arbitrary"), est),
    )(a, *[w] * len(w_specs), x, g_next)


def _final_norm_kernel(x_ref, g_ref, o_ref):
    x = x_ref[...]
    ms = jnp.mean(x * x, axis=-1, keepdims=True)
    o_ref[...] = (x * lax.rsqrt(ms + EPS)) * g_ref[...]


def _final_norm(x, g):
    m, d = x.shape
    bm = _pick(m, (256, 128, 64, 32, 16, 8))
    return pl.pallas_call(
        _final_norm_kernel,
        grid=(m // bm,),
        in_specs=[pl.BlockSpec((bm, d), lambda i: (i, 0)), pl.BlockSpec((1, d), lambda i: (0, 0))],
        out_specs=pl.BlockSpec((bm, d), lambda i: (i, 0)),
        out_shape=jax.ShapeDtypeStruct((m, d), F32),
        compiler_params=_params(("parallel",), 4 * bm * d * 4),
    )(x, g)


def _pool_prompt_kernel(a_ref, halo_ref, wp_ref, sc_ref, y_ref, ext_scr, *, tiles_per_seq):
    tp = a_ref.shape[0]
    cg = wp_ref.shape[1]
    tile = pl.program_id(0) % tiles_per_seq
    ext_scr[0:POOL_HALO, :] = jnp.where(tile == 0, 0.0, halo_ref[...])
    ext_scr[POOL_HALO:, :] = a_ref[...]
    pos = tile * tp + lax.broadcasted_iota(jnp.int32, (tp, 1), 0)
    for g, w in enumerate(POOL_WINDOWS):
        sl = slice(g * cg, (g + 1) * cg)
        s = ext_scr[POOL_HALO:POOL_HALO + tp, sl]
        for k in range(1, w):
            s = s + ext_scr[POOL_HALO - k:POOL_HALO - k + tp, sl]
        cnt = jnp.minimum(pos + 1, w).astype(F32)
        dlt = s / cnt - a_ref[:, sl]
        y = _dot(dlt.astype(BF16), wp_ref[g]) * sc_ref[:, sl]
        y_ref[:, sl] = y.astype(y_ref.dtype)


def _pool_prompt(z, seq, w_pool, scale, layer):
    m = z.shape[0]
    wa = scale.shape[2]
    tp = _pick(seq, (512, 256, 128, 64, 32, 16))
    tiles_per_seq = seq // tp
    halo_blocks = tp // POOL_HALO
    est = (2 * tp * wa * 4 + 2 * POOL_HALO * wa * 4 + (tp + POOL_HALO) * wa * 4 + 2 * tp * wa * 2
           + 4 * wa * wa // N_POOL)
    return pl.pallas_call(
        functools.partial(_pool_prompt_kernel, tiles_per_seq=tiles_per_seq),
        grid=(m // tp,),
        in_specs=[
            pl.BlockSpec((tp, wa), lambda i: (i, 0)),
            pl.BlockSpec((POOL_HALO, wa), lambda i: (jnp.maximum(i * halo_blocks - 1, 0), 0)),
            _layer_spec(w_pool, layer),
            _layer_spec(scale, layer),
        ],
        out_specs=pl.BlockSpec((tp, wa), lambda i: (i, 0)),
        out_shape=jax.ShapeDtypeStruct((m, wa), BF16),
        scratch_shapes=[pltpu.VMEM((tp + POOL_HALO, wa), F32)],
        compiler_params=_params(("parallel",), est),
    )(z, z, w_pool, scale)


def _pool_sample_kernel(hist_ref, new_ref, wp_ref, sc_ref, y_ref, *, first_pos):
    cg = wp_ref.shape[1]
    n_new = new_ref.shape[0]
    rows = [hist_ref[r] for r in range(POOL_BUF)] + [new_ref[t] for t in range(n_new)]
    for t in range(n_new):
        for g, w in enumerate(POOL_WINDOWS):
            sl = slice(g * cg, (g + 1) * cg)
            s = rows[POOL_BUF + t][:, sl]
            for k in range(1, w):
                s = s + rows[POOL_BUF + t - k][:, sl]
            cnt = float(min(first_pos + t + 1, w))
            dlt = s / cnt - rows[POOL_BUF + t][:, sl]
            y = _dot(dlt.astype(BF16), wp_ref[g]) * sc_ref[:, sl]
            y_ref[t, :, sl] = y.astype(y_ref.dtype)


def _pool_sample(hist, new, w_pool, scale, first_pos, layer):
    n_new, b, wa = new.shape
    bb = _pick(b, (32, 16, 8))
    est = 2 * (POOL_BUF + 2 * n_new) * bb * wa * 4 + 4 * wa * wa // N_POOL
    return pl.pallas_call(
        functools.partial(_pool_sample_kernel, first_pos=first_pos),
        grid=(b // bb,),
        in_specs=[
            pl.BlockSpec((None, POOL_BUF, bb, wa), lambda i: (layer, 0, i, 0)),
            pl.BlockSpec((n_new, bb, wa), lambda i: (0, i, 0)),
            _layer_spec(w_pool, layer),
            _layer_spec(scale, layer),
        ],
        out_specs=pl.BlockSpec((n_new, bb, wa), lambda i: (0, i, 0)),
        out_shape=jax.ShapeDtypeStruct((n_new, b, wa), F32),
        compiler_params=_params(("parallel",), est),
    )(hist, new, w_pool, scale)


def _chunk_mlp_kernel(u_ref, v_ref, ws_ref, b_ref, gv_ref, yc_ref, vn_ref, *, seq_rows):
    rows = u_ref.shape[0]
    cg = u_ref.shape[1] // N_CM
    ri = lax.broadcasted_iota(jnp.int32, (rows, rows), 0)
    ci = lax.broadcasted_iota(jnp.int32, (rows, rows), 1)
    mask = ci <= ri
    if seq_rows < rows:
        mask = mask & (_seq_of(ri, seq_rows) == _seq_of(ci, seq_rows))
    for g in range(N_CM):
        sl = slice(g * cg, (g + 1) * cg)
        u = _gelu_tanh(u_ref[:, sl])
        v = _gelu_tanh(v_ref[:, sl])
        ms = jnp.mean(v * v, axis=-1, keepdims=True)
        vn = (v * lax.rsqrt(ms + EPS)) * gv_ref[:, sl]
        vn_ref[:, sl] = vn
        wm = jnp.where(mask, ws_ref[g], 0.0).astype(BF16)
        mix = _dot(wm, vn.astype(BF16)) + b_ref[:, g:g + 1]
        yc_ref[:, sl] = (u * mix).astype(yc_ref.dtype)


def _chunk_mlp(z, u_blk, w_mix, bias, g_v, seq_rows, layer):
    m = z.shape[0]
    wc = g_v.shape[2]
    tiles_per_seq = max(seq_rows // CHUNK, 1)
    seq_rows = min(seq_rows, CHUNK)
    assert m % (tiles_per_seq * CHUNK) == 0
    est = 4 * CHUNK * wc * 4 + 2 * N_CM * CHUNK * CHUNK * 4 + 2 * CHUNK * wc * 2 + 2 * CHUNK * wc * 4
    return pl.pallas_call(
        functools.partial(_chunk_mlp_kernel, seq_rows=seq_rows),
        grid=(m // CHUNK,),
        in_specs=[
            pl.BlockSpec((CHUNK, wc), lambda i: (i, u_blk)),
            pl.BlockSpec((CHUNK, wc), lambda i: (i, u_blk + 1)),
            _layer_spec(w_mix, layer),
            _layer_spec(bias, layer),
            _layer_spec(g_v, layer),
        ],
        out_specs=[pl.BlockSpec((CHUNK, wc), lambda i: (i, 0)),
                   pl.BlockSpec((CHUNK, wc), lambda i: (i // tiles_per_seq, 0))],
        out_shape=[jax.ShapeDtypeStruct((m, wc), BF16), jax.ShapeDtypeStruct((m // tiles_per_seq, wc), F32)],
        compiler_params=_params(("arbitrary",), est),
    )(z, z, w_mix, bias, g_v)


def _col_to_row(col, eye):
    return jnp.sum(jnp.where(eye, col, 0.0), axis=0, keepdims=True)


def _head_blocks(refs, h, width):
    r = refs[h // 2]
    return r[:, (h % 2) * width:(h % 2 + 1) * width]


def _mlstm_gates(zg_ref, bias_ref, h, same, causal, eye):
    i_col = zg_ref[:, h:h + 1] + bias_ref[:, h:h + 1]
    f_col = zg_ref[:, MLSTM_HEADS + h:MLSTM_HEADS + h + 1] + bias_ref[:, MLSTM_HEADS + h:MLSTM_HEADS + h + 1]
    lf_col = _log_sigmoid(f_col)
    lf_row = _col_to_row(lf_col, eye)
    i_row = _col_to_row(i_col, eye)
    fcum_col = jnp.sum(jnp.where(causal, lf_row, 0.0), axis=1, keepdims=True)
    fcum_row = _col_to_row(fcum_col, eye)
    ftot_col = jnp.sum(jnp.where(same, lf_row, 0.0), axis=1, keepdims=True)
    return i_col, i_row, fcum_col, fcum_row, ftot_col


def _mlstm_prompt_kernel(q_ref, k_ref, v0_ref, v1_ref, o0_ref, o1_ref, zg_ref, bias_ref, gm_ref,
                         yb_ref, c_ref, n_ref, m_ref, c_scr, n_scr, m_scr):
    L = q_ref.shape[0]
    dk = q_ref.shape[1] // MLSTM_HEADS
    dv = 2 * v0_ref.shape[1] // MLSTM_HEADS
    step = pl.program_id(1)

    @pl.when(step == 0)
    def _():
        c_scr[...] = jnp.zeros_like(c_scr)
        n_scr[...] = jnp.zeros_like(n_scr)
        m_scr[...] = jnp.zeros_like(m_scr)

    ri = lax.broadcasted_iota(jnp.int32, (L, L), 0)
    ci = lax.broadcasted_iota(jnp.int32, (L, L), 1)
    causal = ci <= ri
    eye = ci == ri
    same = ci >= 0
    for h in range(MLSTM_HEADS):
        q = q_ref[:, h * dk:(h + 1) * dk] * (dk ** -0.5)
        k = k_ref[:, h * dk:(h + 1) * dk]
        v = _head_blocks((v0_ref, v1_ref), h, dv)
        o = _head_blocks((o0_ref, o1_ref), h, dv)
        i_col, i_row, fcum_col, fcum_row, ftot = _mlstm_gates(zg_ref, bias_ref, h, same, causal, eye)
        m_prev = m_scr[h:h + 1, 0:1]
        dlog = jnp.where(causal, fcum_col - fcum_row + i_row, -jnp.inf)
        m_inter = m_prev + fcum_col
        m_tok = jnp.maximum(m_inter, jnp.max(dlog, axis=1, keepdims=True))
        qb = q.astype(BF16)
        kb = k.astype(BF16)
        vb = v.astype(BF16)
        s = _dot_nt(qb, kb) * jnp.exp(dlog - m_tok)
        w_inter = jnp.exp(m_inter - m_tok)
        cmat = c_scr[h]
        nvec = n_scr[h:h + 1, :]
        num = _dot(s.astype(BF16), vb) + w_inter * _dot(qb, cmat.astype(BF16))
        den = jnp.sum(s, axis=1, keepdims=True) + w_inter * jnp.sum(q * nvec, axis=1, keepdims=True)
        den = jnp.maximum(jnp.abs(den), jnp.exp(-m_tok))
        hh = num / den
        ms = jnp.mean(hh * hh, axis=-1, keepdims=True)
        yn = (hh * lax.rsqrt(ms + EPS)) * gm_ref[:, h * dv:(h + 1) * dv]
        yb_ref[:, h * dv:(h + 1) * dv] = (yn * jax.nn.sigmoid(o)).astype(yb_ref.dtype)
        wlog = ftot - fcum_col + i_col
        m_new = jnp.maximum(m_prev + ftot[0:1, :], jnp.max(wlog, axis=0, keepdims=True))
        wk = jnp.exp(wlog - m_new)
        decay = jnp.exp(m_prev + ftot[0:1, :] - m_new)
        kw = k * wk
        c_scr[h] = decay * cmat + lax.dot_general(kw.astype(BF16), vb, (((0,), (0,)), ((), ())),
                                                  preferred_element_type=F32)
        n_scr[h:h + 1, :] = decay * nvec + jnp.sum(kw, axis=0, keepdims=True)
        m_scr[h:h + 1, :] = jnp.broadcast_to(m_new, (1, m_scr.shape[1]))

    @pl.when(step == pl.num_programs(1) - 1)
    def _():
        c_ref[0] = c_scr[...]
        n_ref[0] = n_scr[...]
        m_ref[0] = m_scr[0:MLSTM_HEADS, 0:1]


def _mlstm_prompt(z, zg, bias, g_m, batch, seq, q_blk, layer):
    m = z.shape[0]
    wb = g_m.shape[2]
    qw = wb // 2
    dk = qw // MLSTM_HEADS
    dv = wb // MLSTM_HEADS
    L = _pick(seq, (256, 128, 64, 32, 16, 8))
    nc = seq // L
    row = lambda b, c: b * nc + c
    zspec = lambda blk: pl.BlockSpec((L, qw), lambda b, c: (row(b, c), blk))
    est = 2 * 6 * L * qw * 4 + 2 * L * GATE_LANES * 4 + 2 * L * wb * 2 + 3 * MLSTM_HEADS * dk * dv * 4 + 16 * L * L * 4
    return pl.pallas_call(
        _mlstm_prompt_kernel,
        grid=(batch, nc),
        in_specs=[zspec(q_blk + i) for i in range(6)] + [
            pl.BlockSpec((L, GATE_LANES), lambda b, c: (row(b, c), 0)),
            _layer_spec(bias, layer),
            _layer_spec(g_m, layer),
        ],
        out_specs=[
            pl.BlockSpec((L, wb), lambda b, c: (row(b, c), 0)),
            pl.BlockSpec((1, MLSTM_HEADS, dk, dv), lambda b, c: (b, 0, 0, 0)),
            pl.BlockSpec((1, MLSTM_HEADS, dk), lambda b, c: (b, 0, 0)),
            pl.BlockSpec((1, MLSTM_HEADS, 1), lambda b, c: (b, 0, 0)),
        ],
        out_shape=[
            jax.ShapeDtypeStruct((m, wb), BF16),
            jax.ShapeDtypeStruct((batch, MLSTM_HEADS, dk, dv), F32),
            jax.ShapeDtypeStruct((batch, MLSTM_HEADS, dk), F32),
            jax.ShapeDtypeStruct((batch, MLSTM_HEADS, 1), F32),
        ],
        scratch_shapes=[
            pltpu.VMEM((MLSTM_HEADS, dk, dv), F32),
            pltpu.VMEM((MLSTM_HEADS, dk), F32),
            pltpu.VMEM((8, 128), F32),
        ],
        compiler_params=_params(("parallel", "arbitrary"), est),
    )(z, z, z, z, z, z, zg, bias, g_m)


def _mlstm_sample_kernel(q_ref, k_ref, v0_ref, v1_ref, o0_ref, o1_ref, zg_ref, bias_ref, gm_ref,
                         c0_ref, nrow_ref, mrow_ref, n0_ref, *rest, seq_rows):
    yb_ref, c_ref, n_ref, m_ref = rest[-4:]
    R = q_ref.shape[0]
    nb = R // seq_rows
    dk = q_ref.shape[1] // MLSTM_HEADS
    dv = 2 * v0_ref.shape[1] // MLSTM_HEADS
    ri = lax.broadcasted_iota(jnp.int32, (R, R), 0)
    ci = lax.broadcasted_iota(jnp.int32, (R, R), 1)
    same = _seq_of(ri, seq_rows) == _seq_of(ci, seq_rows)
    causal = same & (ci <= ri)
    eye = ci == ri
    rid = lax.broadcasted_iota(jnp.int32, (R, 1), 0)
    for h in range(MLSTM_HEADS):
        q = q_ref[:, h * dk:(h + 1) * dk] * (dk ** -0.5)
        k = k_ref[:, h * dk:(h + 1) * dk]
        v = _head_blocks((v0_ref, v1_ref), h, dv)
        o = _head_blocks((o0_ref, o1_ref), h, dv)
        i_col, i_row, fcum_col, fcum_row, ftot = _mlstm_gates(zg_ref, bias_ref, h, same, causal, eye)
        m_prev = mrow_ref[:, h:h + 1]
        dlog = jnp.where(causal, fcum_col - fcum_row + i_row, -jnp.inf)
        m_inter = m_prev + fcum_col
        m_tok = jnp.maximum(m_inter, jnp.max(dlog, axis=1, keepdims=True))
        qb = q.astype(BF16)
        kb = k.astype(BF16)
        vb = v.astype(BF16)
        s = _dot_nt(qb, kb) * jnp.exp(dlog - m_tok)
        w_inter = jnp.exp(m_inter - m_tok)
        inter = jnp.zeros((R, dv), F32)
        for b in range(nb):
            mine = _seq_of(rid, seq_rows) == b
            inter = inter + jnp.where(mine, _dot(qb, c0_ref[b, h].astype(BF16)), 0.0)
        num = _dot(s.astype(BF16), vb) + w_inter * inter
        nrow = nrow_ref[:, h * dk:(h + 1) * dk]
        den = jnp.sum(s, axis=1, keepdims=True) + w_inter * jnp.sum(q * nrow, axis=1, keepdims=True)
        den = jnp.maximum(jnp.abs(den), jnp.exp(-m_tok))
        hh = num / den
        ms = jnp.mean(hh * hh, axis=-1, keepdims=True)
        yn = (hh * lax.rsqrt(ms + EPS)) * gm_ref[:, h * dv:(h + 1) * dv]
        yb_ref[:, h * dv:(h + 1) * dv] = (yn * jax.nn.sigmoid(o)).astype(yb_ref.dtype)
        wlog_col = ftot - fcum_col + i_col
        wlog_row = _col_to_row(wlog_col, eye)
        m_new = jnp.maximum(m_prev + ftot, jnp.max(jnp.where(same, wlog_row, -jnp.inf), axis=1, keepdims=True))
        wk = jnp.exp(wlog_col - m_new)
        decay = jnp.exp(m_prev + ftot - m_new)
        kw = k * wk
        for b in range(nb):
            mine = _seq_of(rid, seq_rows) == b
            kw_b = jnp.where(mine, kw, 0.0)
            r0 = b * seq_rows
            dec_b = decay[r0:r0 + 1, :]
            upd = lax.dot_general(kw_b.astype(BF16), vb, (((0,), (0,)), ((), ())), preferred_element_type=F32)
            c_ref[b, h] = dec_b * c0_ref[b, h] + upd
            n_ref[b, h:h + 1, :] = dec_b * n0_ref[b, h:h + 1, :] + jnp.sum(kw_b, axis=0, keepdims=True)
            m_ref[0, b:b + 1, h:h + 1] = m_new[r0:r0 + 1, :]


def _mlstm_sample(z, zg, bias, g_m, c0, n0, n_rows, m_rows, c_acc, seq_rows, q_blk, layer):
    m = z.shape[0]
    depth, batch = c0.shape[:2]
    wb = g_m.shape[2]
    qw = wb // 2
    dk = qw // MLSTM_HEADS
    dv = wb // MLSTM_HEADS
    nb = 4
    R = nb * seq_rows
    zspec = lambda blk: pl.BlockSpec((R, qw), lambda s: (s, blk))
    cblk = nb * MLSTM_HEADS * dk * dv * 4
    est = 4 * cblk + 2 * 7 * R * qw * 4 + 2 * R * GATE_LANES * 4 + 2 * R * wb * 2
    in_specs = [zspec(q_blk + i) for i in range(6)] + [
        pl.BlockSpec((R, GATE_LANES), lambda s: (s, 0)),
        _layer_spec(bias, layer),
        _layer_spec(g_m, layer),
        pl.BlockSpec((None, nb, MLSTM_HEADS, dk, dv), lambda s: (layer, s, 0, 0, 0)),
        pl.BlockSpec((None, R, qw), lambda s: (layer, s, 0)),
        pl.BlockSpec((None, R, MLSTM_HEADS), lambda s: (layer, s, 0)),
        pl.BlockSpec((None, nb, MLSTM_HEADS, dk), lambda s: (layer, s, 0, 0)),
    ]
    args = [z, z, z, z, z, z, zg, bias, g_m, c0, n_rows, m_rows, n0]
    aliases = {}
    if c_acc is not None:
        in_specs.append(pl.BlockSpec(memory_space=pl.ANY))
        aliases = {len(args): 1}
        args.append(c_acc)
    yb, c_all, n1, m1 = pl.pallas_call(
        functools.partial(_mlstm_sample_kernel, seq_rows=seq_rows),
        grid=(m // R,),
        in_specs=in_specs,
        out_specs=[
            pl.BlockSpec((R, wb), lambda s: (s, 0)),
            pl.BlockSpec((None, nb, MLSTM_HEADS, dk, dv), lambda s: (layer, s, 0, 0, 0)),
            pl.BlockSpec((nb, MLSTM_HEADS, dk), lambda s: (s, 0, 0)),
            pl.BlockSpec((1, nb, MLSTM_HEADS), lambda s: (s, 0, 0)),
        ],
        out_shape=[
            jax.ShapeDtypeStruct((m, wb), BF16),
            jax.ShapeDtypeStruct(c0.shape, F32),
            jax.ShapeDtypeStruct(n0.shape[1:], F32),
            jax.ShapeDtypeStruct((batch // nb, nb, MLSTM_HEADS), F32),
        ],
        input_output_aliases=aliases,
        compiler_params=_params(("parallel",), est),
    )(*args)
    return yb, c_all, n1, m1.reshape(batch, MLSTM_HEADS)


def kernel(x_prompt, x_sample, state_pool, state_mlstm_c, state_mlstm_n, state_mlstm_m, p_prompt, p_sample, g_mix, w_in, b_igate, b_fgate, w_pool, pool_scale, g_mlstm, w_s, b_s, g_v, w_out, g_ffn, w_ffn_gate, w_ffn_up, w_ffn_down, g_ple, w_ple_gate, w_ple_proj, g_final):
    batch, seq, d = x_prompt.shape
    dec_batch, dec_seq, _ = x_sample.shape
    depth = w_in.shape[0]
    wa = pool_scale.shape[1]
    wb = g_mlstm.shape[1]
    wc = g_v.shape[1]
    n_gates = 2 * MLSTM_HEADS
    off_gate = wa + 2 * (wb // 2) + 2 * wb
    off_c = off_gate + n_gates
    q_blk = wa // (wb // 2)
    u_blk = off_gate // wc

    w_in_t = jnp.swapaxes(w_in, 1, 2)
    wt_gate = jnp.pad(w_in_t[:, off_gate:off_c], ((0, 0), (0, GATE_LANES - n_gates), (0, 0)))
    gate_bias = jnp.pad(jnp.concatenate([b_igate, b_fgate], axis=1), ((0, 0), (0, GATE_LANES - n_gates)))[:, None, :]
    w_pool_b = w_pool.astype(BF16)
    w_fd = w_ffn_down.astype(BF16)
    reps = CHUNK // dec_seq
    ws_sample = jnp.tile(w_s[:, :, :dec_seq, :dec_seq], (1, 1, reps, reps))
    bias_prompt = jnp.swapaxes(b_s, 1, 2)
    bias_sample = jnp.tile(jnp.swapaxes(b_s[:, :, :dec_seq], 1, 2), (1, reps, 1))
    vec = lambda a: a[:, None, :]
    g_m3, g_v3, scale3, g_mix3, g_ffn3, g_ple3 = map(vec, (g_mlstm, g_v, pool_scale, g_mix, g_ffn, g_ple))
    g_fin3 = g_final[None, None, :]

    def next_mix(i):
        return (g_mix3, i + 1) if i + 1 < depth else (g_fin3, 0)

    pp = p_prompt.reshape(depth, batch * seq, -1)
    ps = p_sample.reshape(depth, dec_batch * dec_seq, -1)
    hist_tm = jnp.swapaxes(state_pool, 1, 2)
    n_rows = jnp.repeat(state_mlstm_n.reshape(depth, dec_batch, -1), dec_seq, axis=1)
    m_rows = jnp.repeat(state_mlstm_m, dec_seq, axis=1)

    xp = x_prompt.reshape(batch * seq, d)
    xs = x_sample.reshape(dec_batch * dec_seq, d)
    xpb, ssp = _prep_rows(xp, g_mix3, 0)
    xsb, sss = _prep_rows(xs, g_mix3, 0)
    v_start = ((seq - 1) // CHUNK) * CHUNK
    outs = {k: [] for k in ("pool_p", "pool_s", "c_p", "n_p", "n_s", "m_p", "m_s", "v_p", "v_s")}
    c_s = None

    for i in range(depth):
        zp, zgp = _inproj(xpb, ssp, w_in_t, wt_gate, off_gate, n_gates, i)
        ya = _pool_prompt(zp, seq, w_pool_b, scale3, i)
        yb, c1, n1, m1 = _mlstm_prompt(zp, zgp, gate_bias, g_m3, batch, seq, q_blk, i)
        yc, vn = _chunk_mlp(zp, u_blk, w_s, bias_prompt, g_v3, seq, i)
        xp, xpb, ssp = _wout(ya, yb, yc, w_out, xp, i, g_ffn3, i)
        xp, xpb, ssp = _ffn_down(_ffn_up(xpb, ssp, w_ffn_gate, w_ffn_up, i), w_fd, xp, i, g_ple3, i)
        xp, xpb, ssp = _ple(xpb, ssp, w_ple_gate, pp, w_ple_proj, xp, i, *next_mix(i))
        outs["pool_p"].append(zp.reshape(batch, seq, -1)[:, seq - POOL_BUF:, :wa])
        outs["c_p"].append(c1)
        outs["n_p"].append(n1)
        outs["m_p"].append(m1.reshape(batch, MLSTM_HEADS))
        outs["v_p"].append(vn.reshape(batch, seq - v_start, wc))
        zs, zgs = _inproj(xsb, sss, w_in_t, wt_gate, off_gate, n_gates, i)
        new_tm = jnp.swapaxes(zs[:, :wa].reshape(dec_batch, dec_seq, wa), 0, 1)
        ya_tm = _pool_sample(hist_tm, new_tm, w_pool_b, scale3, PAST_LEN, i)
        ya = jnp.swapaxes(ya_tm, 0, 1).reshape(dec_batch * dec_seq, wa)
        yb, c_s, n1, m1 = _mlstm_sample(zs, zgs, gate_bias, g_m3, state_mlstm_c, state_mlstm_n, n_rows, m_rows,
                                        c_s, dec_seq, q_blk, i)
        yc, vn = _chunk_mlp(zs, u_blk, ws_sample, bias_sample, g_v3, dec_seq, i)
        xs, xsb, sss = _wout(ya, yb, yc, w_out, xs, i, g_ffn3, i)
        xs, xsb, sss = _ffn_down(_ffn_up(xsb, sss, w_ffn_gate, w_ffn_up, i), w_fd, xs, i, g_ple3, i)
        xs, xsb, sss = _ple(xsb, sss, w_ple_gate, ps, w_ple_proj, xs, i, *next_mix(i))
        outs["pool_s"].append(jnp.concatenate([hist_tm[i], new_tm], axis=0)[dec_seq:])
        outs["n_s"].append(n1)
        outs["m_s"].append(m1)
        outs["v_s"].append(vn.reshape(dec_batch, dec_seq, wc))

    y_prompt = _final_norm(xp, g_final[None, :]).reshape(batch, seq, d)
    y_sample = _final_norm(xs, g_final[None, :]).reshape(dec_batch, dec_seq, d)
    st = lambda k: jnp.stack(outs[k])
    pool_sample = jnp.swapaxes(st("pool_s"), 1, 2)
    return (y_prompt, y_sample, st("pool_p"), pool_sample, st("c_p"), c_s, st("n_p"), st("n_s"),
            st("m_p"), st("m_s"), st("v_p"), st("v_s"))
```

```python
import functools
import math

import jax
import jax.numpy as jnp
from jax import lax
from jax.experimental import pallas as pl
from jax.experimental.pallas import tpu as pltpu

F32 = jnp.float32
BF16 = jnp.bfloat16

EPS = 1e-6
PAST_LEN = 16384
POOL_WINDOWS = (2, 4, 8, 16)
POOL_BUF = max(POOL_WINDOWS) - 1
POOL_HALO = 16
N_POOL = len(POOL_WINDOWS)
MLSTM_HEADS = 4
N_CM = 4
CHUNK = 128
GATE_LANES = 128
SS_LANES = 128

VMEM_SLACK_BYTES = 10 * 2**20
VMEM_CAP_BYTES = 58 * 2**20

_ROW_TILES = (1024, 512, 256, 128, 64, 32, 16, 8)


def _params(sem, vmem_estimate):
    limit = min(int(vmem_estimate) + VMEM_SLACK_BYTES, VMEM_CAP_BYTES)
    return pltpu.CompilerParams(dimension_semantics=sem, vmem_limit_bytes=limit)


def _pick(n, candidates):
    for c in candidates:
        if n % c == 0:
            return c
    return n


def _layer_spec(arr, layer):
    nd = arr.ndim - 1
    return pl.BlockSpec((None,) + arr.shape[1:], lambda *_: (layer,) + (0,) * nd)


def _dot(a, b):
    return jnp.dot(a, b, preferred_element_type=F32)


def _dot_nt(a, bt):
    return lax.dot_general(a, bt, (((1,), (1,)), ((), ())), preferred_element_type=F32)


def _gelu_tanh(x):
    return 0.5 * x * (1.0 + jnp.tanh(math.sqrt(2.0 / math.pi) * (x + 0.044715 * (x * x * x))))


def _seq_of(idx, seq_rows):
    if seq_rows & (seq_rows - 1) == 0:
        return lax.shift_right_logical(idx, seq_rows.bit_length() - 1)
    return idx // seq_rows


def _log_sigmoid(x):
    return jnp.minimum(x, 0.0) - jnp.log(1.0 + jnp.exp(-jnp.abs(x)))


def _row_scale(ss_ref, width):
    return lax.rsqrt(ss_ref[:, 0:1] * (1.0 / width) + EPS)


def _gain_spec(g, g_idx):
    return pl.BlockSpec((None,) + g.shape[1:], lambda i, j: (g_idx, 0, 0))


MXU_COLS = 256


def _col_groups(bn):
    w = MXU_COLS if bn % MXU_COLS == 0 else bn
    return [slice(c, c + w) for c in range(0, bn, w)]


def _group_specs(n_rows, bn, layer):
    groups = _col_groups(bn)
    gw = bn // len(groups)
    return [pl.BlockSpec((None, n_rows, gw), lambda i, j, g=g: (layer, 0, j * len(groups) + g))
            for g in range(len(groups))]


def _emit_rows(new_cols, j, gn_ref, o_ref, xb_ref, ss_ref):
    bn = o_ref.shape[1]
    sq = None
    for grp, cols in enumerate(_col_groups(bn)):
        xn = new_cols(grp, cols)
        o_ref[:, cols] = xn
        g = gn_ref[:, pl.ds(pl.multiple_of(j * bn + cols.start, cols.stop - cols.start), cols.stop - cols.start)]
        xb_ref[:, cols] = (xn * g).astype(xb_ref.dtype)
        s = jnp.sum(xn * xn, axis=-1, keepdims=True)
        sq = s if sq is None else sq + s
    part = jnp.broadcast_to(sq, ss_ref.shape)

    @pl.when(j == 0)
    def _():
        ss_ref[...] = part

    @pl.when(j > 0)
    def _():
        ss_ref[...] += part


def _row_outs(m, d, bm, bn):
    specs = [
        pl.BlockSpec((bm, bn), lambda i, j: (i, j)),
        pl.BlockSpec((bm, bn), lambda i, j: (i, j)),
        pl.BlockSpec((bm, SS_LANES), lambda i, j: (i, 0)),
    ]
    shapes = [jax.ShapeDtypeStruct((m, d), F32), jax.ShapeDtypeStruct((m, d), BF16),
              jax.ShapeDtypeStruct((m, SS_LANES), F32)]
    return specs, shapes


def _prep_rows_kernel(x_ref, gn_ref, xb_ref, ss_ref):
    x = x_ref[...]
    xb_ref[...] = (x * gn_ref[...]).astype(xb_ref.dtype)
    ss_ref[...] = jnp.broadcast_to(jnp.sum(x * x, axis=-1, keepdims=True), ss_ref.shape)


def _prep_rows(x, g, g_idx):
    m, d = x.shape
    bm = _pick(m, (256, 128, 64, 32, 16, 8))
    return pl.pallas_call(
        _prep_rows_kernel,
        grid=(m // bm,),
        in_specs=[pl.BlockSpec((bm, d), lambda i: (i, 0)), _layer_spec(g, g_idx)],
        out_specs=[pl.BlockSpec((bm, d), lambda i: (i, 0)), pl.BlockSpec((bm, SS_LANES), lambda i: (i, 0))],
        out_shape=[jax.ShapeDtypeStruct((m, d), BF16), jax.ShapeDtypeStruct((m, SS_LANES), F32)],
        compiler_params=_params(("parallel",), 2 * bm * d * 6),
    )(x, g)


def _inproj_kernel(xb_ref, ss_ref, *refs, with_gates):
    if with_gates:
        wgt_ref, z_ref, zg_ref = refs[-3:]
        wt_refs = refs[:-3]
    else:
        z_ref = refs[-1]
        wt_refs = refs[:-1]
    rs = _row_scale(ss_ref, xb_ref.shape[1])

    if with_gates:
        @pl.when(pl.program_id(1) == 0)
        def _():
            zg_ref[...] = rs * _dot_nt(xb_ref[...], wgt_ref[...].astype(BF16))

    for grp, cols in enumerate(_col_groups(z_ref.shape[1])):
        z_ref[:, cols] = (rs * _dot_nt(xb_ref[...], wt_refs[grp][0].astype(BF16))).astype(z_ref.dtype)


def _inproj(xb, ss, wt, row0, n, out_dtype, layer, skip_at=None, n_skip=0, wt_gate=None):
    m, d = xb.shape
    bm = _pick(m, _ROW_TILES)
    lead = n if skip_at is None else skip_at
    bn = _pick(math.gcd(lead, n), (512, 256, 128))
    lead_blocks = lead // bn
    assert row0 % 8 == 0 and n_skip % 8 == 0 and lead % bn == 0 and n % bn == 0
    itemsize = jnp.dtype(out_dtype).itemsize
    est = 2 * bm * d * 2 + 2 * d * bn * 4 + d * bn * 2 + 2 * bm * bn * itemsize + bm * bn * 4
    groups = _col_groups(bn)
    gw = bn // len(groups)

    def rows_of(grp):
        def index(i, j):
            row = row0 + j * bn + grp * gw + jnp.where(j >= lead_blocks, n_skip, 0)
            return (layer, pl.multiple_of(row, 8), 0)
        return pl.BlockSpec((pl.Element(1), pl.Element(gw), pl.Element(d)), index)

    in_specs = [
        pl.BlockSpec((bm, d), lambda i, j: (i, 0)),
        pl.BlockSpec((bm, SS_LANES), lambda i, j: (i, 0)),
        *[rows_of(grp) for grp in range(len(groups))],
    ]
    args = [xb, ss, *[wt] * len(groups)]
    out_specs = [pl.BlockSpec((bm, bn), lambda i, j: (i, j))]
    out_shape = [jax.ShapeDtypeStruct((m, n), out_dtype)]
    if wt_gate is not None:
        in_specs.append(_layer_spec(wt_gate, layer))
        args.append(wt_gate)
        out_specs.append(pl.BlockSpec((bm, GATE_LANES), lambda i, j: (i, 0)))
        out_shape.append(jax.ShapeDtypeStruct((m, GATE_LANES), F32))
        est += 2 * d * GATE_LANES * 4 + 4 * bm * GATE_LANES * 4
    outs = pl.pallas_call(
        functools.partial(_inproj_kernel, with_gates=wt_gate is not None),
        grid=(m // bm, n // bn),
        in_specs=in_specs,
        out_specs=out_specs,
        out_shape=out_shape,
        compiler_params=_params(("parallel", "arbitrary"), est),
    )(*args)
    return outs if wt_gate is not None else outs[0]


def _ffn_up_kernel(xb_ref, ss_ref, wg_ref, wu_ref, o_ref):
    rs = _row_scale(ss_ref, xb_ref.shape[1])
    h = xb_ref[...]
    a = rs * _dot(h, wg_ref[...].astype(BF16))
    b = rs * _dot(h, wu_ref[...].astype(BF16))
    o_ref[...] = ((a * jax.nn.sigmoid(a)) * b).astype(o_ref.dtype)


def _ffn_up(xb, ss, w_gate, w_up, layer):
    m, d = xb.shape
    n = w_gate.shape[2]
    bm = _pick(m, _ROW_TILES)
    tb = _pick(n, (256, 128))
    est = 2 * bm * d * 2 + 4 * d * tb * 4 + 2 * d * tb * 2 + 2 * bm * tb * 2 + 2 * bm * SS_LANES * 4
    wspec = pl.BlockSpec((None, d, tb), lambda i, j: (layer, 0, j))
    return pl.pallas_call(
        _ffn_up_kernel,
        grid=(m // bm, n // tb),
        in_specs=[pl.BlockSpec((bm, d), lambda i, j: (i, 0)), pl.BlockSpec((bm, SS_LANES), lambda i, j: (i, 0)),
                  wspec, wspec],
        out_specs=pl.BlockSpec((bm, tb), lambda i, j: (i, j)),
        out_shape=jax.ShapeDtypeStruct((m, n), BF16),
        compiler_params=_params(("parallel", "arbitrary"), est),
    )(xb, ss, w_gate, w_up)


def _ple_kernel(xb_ref, ss_ref, *refs):
    p_ref, wpp_ref, x_ref, gn_ref, o_ref, xbo_ref, sso_ref = refs[-7:]
    w_refs = refs[:-7]
    rs = _row_scale(ss_ref, xb_ref.shape[1])
    pb = p_ref[...].astype(BF16)

    def new_cols(grp, cols):
        gate = jax.nn.sigmoid(rs * _dot(xb_ref[...], w_refs[grp][...].astype(BF16)))
        emb = _dot(pb, wpp_ref[:, cols].astype(BF16))
        return x_ref[:, cols] + gate * emb

    _emit_rows(new_cols, pl.program_id(1), gn_ref, o_ref, xbo_ref, sso_ref)


def _ple(xb, ss, w_pg, p, w_pp, x, layer, g_next, g_idx):
    m, d = x.shape
    pd = p.shape[2]
    bm = _pick(m, _ROW_TILES)
    bn = _pick(d, (512, 256, 128))
    est = (2 * bm * d * 2 + 2 * d * bn * 4 + d * bn * 2 + 2 * bm * pd * 4 + 2 * pd * bn * 4 + 4 * bm * bn * 4
           + 2 * bm * bn * 2 + 4 * bm * SS_LANES * 4)
    out_specs, out_shape = _row_outs(m, d, bm, bn)
    w_specs = _group_specs(d, bn, layer)
    return pl.pallas_call(
        _ple_kernel,
        grid=(m // bm, d // bn),
        in_specs=[
            pl.BlockSpec((bm, d), lambda i, j: (i, 0)),
            pl.BlockSpec((bm, SS_LANES), lambda i, j: (i, 0)),
            *w_specs,
            pl.BlockSpec((None, bm, pd), lambda i, j: (layer, i, 0)),
            pl.BlockSpec((None, pd, bn), lambda i, j: (layer, 0, j)),
            pl.BlockSpec((bm, bn), lambda i, j: (i, j)),
            _gain_spec(g_next, g_idx),
        ],
        out_specs=out_specs,
        out_shape=out_shape,
        compiler_params=_params(("parallel", "arbitrary"), est),
    )(xb, ss, *[w_pg] * len(w_specs), p, w_pp, x, g_next)


def _wout_kernel(ya_ref, yb_ref, yc_ref, *refs):
    x_ref, gn_ref, o_ref, xbo_ref, sso_ref = refs[-5:]
    w_refs = refs[:-5]
    wa = ya_ref.shape[1]
    wb = yb_ref.shape[1]
    ya = ya_ref[...].astype(BF16)

    def new_cols(grp, cols):
        w_ref = w_refs[grp]
        xn = x_ref[:, cols] + _dot(ya, w_ref[0:wa, :].astype(BF16))
        xn = xn + _dot(yb_ref[...], w_ref[wa:wa + wb, :].astype(BF16))
        return xn + _dot(yc_ref[...], w_ref[wa + wb:, :].astype(BF16))

    _emit_rows(new_cols, pl.program_id(1), gn_ref, o_ref, xbo_ref, sso_ref)


def _wout(ya, yb, yc, w, x, layer, g_next, g_idx):
    m, d = x.shape
    wa, wb, wc = ya.shape[1], yb.shape[1], yc.shape[1]
    k = wa + wb + wc
    bm = _pick(m, _ROW_TILES)
    bn = _pick(d, (512, 256, 128))
    est = (2 * bm * wa * ya.dtype.itemsize + 2 * bm * wb * 2 + 2 * bm * wc * 2
           + 2 * k * bn * 4 + k * bn * 2 + 4 * bm * bn * 4 + 2 * bm * bn * 2 + 2 * bm * SS_LANES * 4)
    out_specs, out_shape = _row_outs(m, d, bm, bn)
    w_specs = _group_specs(k, bn, layer)
    return pl.pallas_call(
        _wout_kernel,
        grid=(m // bm, d // bn),
        in_specs=[
            pl.BlockSpec((bm, wa), lambda i, j: (i, 0)),
            pl.BlockSpec((bm, wb), lambda i, j: (i, 0)),
            pl.BlockSpec((bm, wc), lambda i, j: (i, 0)),
            *w_specs,
            pl.BlockSpec((bm, bn), lambda i, j: (i, j)),
            _gain_spec(g_next, g_idx),
        ],
        out_specs=out_specs,
        out_shape=out_shape,
        compiler_params=_params(("parallel", "arbitrary"), est),
    )(ya, yb, yc, *[w] * len(w_specs), x, g_next)


def _ffn_down_kernel(a_ref, *refs):
    x_ref, gn_ref, o_ref, xbo_ref, sso_ref = refs[-5:]
    w_refs = refs[:-5]
    _emit_rows(lambda grp, cols: x_ref[:, cols] + _dot(a_ref[...], w_refs[grp][...]),
               pl.program_id(1), gn_ref, o_ref, xbo_ref, sso_ref)


def _ffn_down(a, w, x, layer, g_next, g_idx):
    m, d = x.shape
    k = a.shape[1]
    bm = _pick(m, _ROW_TILES[1:])
    bn = _pick(d, (512, 256, 128))
    est = 2 * bm * k * 2 + 2 * k * bn * 2 + 4 * bm * bn * 4 + 2 * bm * bn * 2 + 2 * bm * SS_LANES * 4
    out_specs, out_shape = _row_outs(m, d, bm, bn)
    w_specs = _group_specs(k, bn, layer)
    return pl.pallas_call(
        _ffn_down_kernel,
        grid=(m // bm, d // bn),
        in_specs=[
            pl.BlockSpec((bm, k), lambda i, j: (i, 0)),
            *w_specs,
            pl.BlockSpec((bm, bn), lambda i, j: (i, j)),
            _gain_spec(g_next, g_idx),
        ],
        out_specs=out_specs,
        out_shape=out_shape,
        compiler_params=_params(("parallel", "arbitrary"), est),
    )(a, *[w] * len(w_specs), x, g_next)


def _final_norm_kernel(x_ref, g_ref, o_ref):
    x = x_ref[...]
    ms = jnp.mean(x * x, axis=-1, keepdims=True)
    o_ref[...] = (x * lax.rsqrt(ms + EPS)) * g_ref[...]


def _final_norm(x, g):
    m, d = x.shape
    bm = _pick(m, (256, 128, 64, 32, 16, 8))
    return pl.pallas_call(
        _final_norm_kernel,
        grid=(m // bm,),
        in_specs=[pl.BlockSpec((bm, d), lambda i: (i, 0)), pl.BlockSpec((1, d), lambda i: (0, 0))],
        out_specs=pl.BlockSpec((bm, d), lambda i: (i, 0)),
        out_shape=jax.ShapeDtypeStruct((m, d), F32),
        compiler_params=_params(("parallel",), 4 * bm * d * 4),
    )(x, g)


def _pool_prompt_kernel(a_ref, halo_ref, wp_ref, sc_ref, y_ref, ext_scr, *, tiles_per_seq):
    tp = a_ref.shape[0]
    cg = wp_ref.shape[1]
    tile = pl.program_id(0) % tiles_per_seq
    ext_scr[0:POOL_HALO, :] = jnp.where(tile == 0, 0.0, halo_ref[...])
    ext_scr[POOL_HALO:, :] = a_ref[...]
    pos = tile * tp + lax.broadcasted_iota(jnp.int32, (tp, 1), 0)
    for g, w in enumerate(POOL_WINDOWS):
        sl = slice(g * cg, (g + 1) * cg)
        s = ext_scr[POOL_HALO:POOL_HALO + tp, sl]
        for k in range(1, w):
            s = s + ext_scr[POOL_HALO - k:POOL_HALO - k + tp, sl]
        cnt = jnp.minimum(pos + 1, w).astype(F32)
        dlt = s / cnt - a_ref[:, sl]
        y = _dot(dlt.astype(BF16), wp_ref[g]) * sc_ref[:, sl]
        y_ref[:, sl] = y.astype(y_ref.dtype)


def _pool_prompt(z, seq, w_pool, scale, layer):
    m = z.shape[0]
    wa = scale.shape[2]
    tp = _pick(seq, (512, 256, 128, 64, 32, 16))
    tiles_per_seq = seq // tp
    halo_blocks = tp // POOL_HALO
    est = (2 * tp * wa * 4 + 2 * POOL_HALO * wa * 4 + (tp + POOL_HALO) * wa * 4 + 2 * tp * wa * 2
           + 4 * wa * wa // N_POOL)
    return pl.pallas_call(
        functools.partial(_pool_prompt_kernel, tiles_per_seq=tiles_per_seq),
        grid=(m // tp,),
        in_specs=[
            pl.BlockSpec((tp, wa), lambda i: (i, 0)),
            pl.BlockSpec((POOL_HALO, wa), lambda i: (jnp.maximum(i * halo_blocks - 1, 0), 0)),
            _layer_spec(w_pool, layer),
            _layer_spec(scale, layer),
        ],
        out_specs=pl.BlockSpec((tp, wa), lambda i: (i, 0)),
        out_shape=jax.ShapeDtypeStruct((m, wa), BF16),
        scratch_shapes=[pltpu.VMEM((tp + POOL_HALO, wa), F32)],
        compiler_params=_params(("parallel",), est),
    )(z, z, w_pool, scale)


def _pool_sample_kernel(hist_ref, new_ref, wp_ref, sc_ref, y_ref, *, first_pos):
    cg = wp_ref.shape[1]
    n_new = new_ref.shape[0]
    rows = [hist_ref[r] for r in range(POOL_BUF)] + [new_ref[t] for t in range(n_new)]
    for t in range(n_new):
        for g, w in enumerate(POOL_WINDOWS):
            sl = slice(g * cg, (g + 1) * cg)
            s = rows[POOL_BUF + t][:, sl]
            for k in range(1, w):
                s = s + rows[POOL_BUF + t - k][:, sl]
            cnt = float(min(first_pos + t + 1, w))
            dlt = s / cnt - rows[POOL_BUF + t][:, sl]
            y = _dot(dlt.astype(BF16), wp_ref[g]) * sc_ref[:, sl]
            y_ref[t, :, sl] = y.astype(y_ref.dtype)


def _pool_sample(hist, new, w_pool, scale, first_pos, layer):
    n_new, b, wa = new.shape
    bb = _pick(b, (32, 16, 8))
    est = 2 * (POOL_BUF + 2 * n_new) * bb * wa * 4 + 4 * wa * wa // N_POOL
    return pl.pallas_call(
        functools.partial(_pool_sample_kernel, first_pos=first_pos),
        grid=(b // bb,),
        in_specs=[
            pl.BlockSpec((None, POOL_BUF, bb, wa), lambda i: (layer, 0, i, 0)),
            pl.BlockSpec((n_new, bb, wa), lambda i: (0, i, 0)),
            _layer_spec(w_pool, layer),
            _layer_spec(scale, layer),
        ],
        out_specs=pl.BlockSpec((n_new, bb, wa), lambda i: (0, i, 0)),
        out_shape=jax.ShapeDtypeStruct((n_new, b, wa), F32),
        compiler_params=_params(("parallel",), est),
    )(hist, new, w_pool, scale)


def _chunk_mlp_kernel(u_ref, v_ref, ws_ref, b_ref, gv_ref, yc_ref, vn_ref, *, seq_rows):
    rows = u_ref.shape[0]
    cg = u_ref.shape[1] // N_CM
    ri = lax.broadcasted_iota(jnp.int32, (rows, rows), 0)
    ci = lax.broadcasted_iota(jnp.int32, (rows, rows), 1)
    mask = ci <= ri
    if seq_rows < rows:
        mask = mask & (_seq_of(ri, seq_rows) == _seq_of(ci, seq_rows))
    for g in range(N_CM):
        sl = slice(g * cg, (g + 1) * cg)
        u = _gelu_tanh(u_ref[:, sl].astype(F32))
        v = _gelu_tanh(v_ref[:, sl].astype(F32))
        ms = jnp.mean(v * v, axis=-1, keepdims=True)
        vn = (v * lax.rsqrt(ms + EPS)) * gv_ref[:, sl]
        vn_ref[:, sl] = vn
        wm = jnp.where(mask, ws_ref[g], 0.0).astype(BF16)
        mix = _dot(wm, vn.astype(BF16)) + b_ref[:, g:g + 1]
        yc_ref[:, sl] = (u * mix).astype(yc_ref.dtype)


def _chunk_mlp(z, u_blk, w_mix, bias, g_v, seq_rows, layer):
    m = z.shape[0]
    wc = g_v.shape[2]
    tiles_per_seq = max(seq_rows // CHUNK, 1)
    seq_rows = min(seq_rows, CHUNK)
    assert m % (tiles_per_seq * CHUNK) == 0
    est = 4 * CHUNK * wc * 4 + 2 * N_CM * CHUNK * CHUNK * 4 + 2 * CHUNK * wc * 2 + 2 * CHUNK * wc * 4
    return pl.pallas_call(
        functools.partial(_chunk_mlp_kernel, seq_rows=seq_rows),
        grid=(m // CHUNK,),
        in_specs=[
            pl.BlockSpec((CHUNK, wc), lambda i: (i, u_blk)),
            pl.BlockSpec((CHUNK, wc), lambda i: (i, u_blk + 1)),
            _layer_spec(w_mix, layer),
            _layer_spec(bias, layer),
            _layer_spec(g_v, layer),
        ],
        out_specs=[pl.BlockSpec((CHUNK, wc), lambda i: (i, 0)),
                   pl.BlockSpec((CHUNK, wc), lambda i: (i // tiles_per_seq, 0))],
        out_shape=[jax.ShapeDtypeStruct((m, wc), BF16), jax.ShapeDtypeStruct((m // tiles_per_seq, wc), F32)],
        compiler_params=_params(("arbitrary",), est),
    )(z, z, w_mix, bias, g_v)


def _col_to_row(col, eye):
    return jnp.sum(jnp.where(eye, col, 0.0), axis=0, keepdims=True)


def _head_blocks(refs, h, width):
    r = refs[h // 2]
    return r[:, (h % 2) * width:(h % 2 + 1) * width]


def _mlstm_gates(zg_ref, bias_ref, h, same, causal, eye):
    i_col = zg_ref[:, h:h + 1] + bias_ref[:, h:h + 1]
    f_col = zg_ref[:, MLSTM_HEADS + h:MLSTM_HEADS + h + 1] + bias_ref[:, MLSTM_HEADS + h:MLSTM_HEADS + h + 1]
    lf_col = _log_sigmoid(f_col)
    lf_row = _col_to_row(lf_col, eye)
    i_row = _col_to_row(i_col, eye)
    fcum_col = jnp.sum(jnp.where(causal, lf_row, 0.0), axis=1, keepdims=True)
    fcum_row = _col_to_row(fcum_col, eye)
    ftot_col = jnp.sum(jnp.where(same, lf_row, 0.0), axis=1, keepdims=True)
    return i_col, i_row, fcum_col, fcum_row, ftot_col


def _mlstm_prompt_kernel(q_ref, k_ref, v0_ref, v1_ref, o0_ref, o1_ref, zg_ref, bias_ref, gm_ref,
                         yb_ref, c_ref, n_ref, m_ref, c_scr, n_scr, m_scr):
    L = q_ref.shape[0]
    dk = q_ref.shape[1] // MLSTM_HEADS
    dv = 2 * v0_ref.shape[1] // MLSTM_HEADS
    step = pl.program_id(1)

    @pl.when(step == 0)
    def _():
        c_scr[...] = jnp.zeros_like(c_scr)
        n_scr[...] = jnp.zeros_like(n_scr)
        m_scr[...] = jnp.zeros_like(m_scr)

    ri = lax.broadcasted_iota(jnp.int32, (L, L), 0)
    ci = lax.broadcasted_iota(jnp.int32, (L, L), 1)
    causal = ci <= ri
    eye = ci == ri
    same = ci >= 0
    for h in range(MLSTM_HEADS):
        q = q_ref[:, h * dk:(h + 1) * dk].astype(F32) * (dk ** -0.5)
        k = k_ref[:, h * dk:(h + 1) * dk]
        v = _head_blocks((v0_ref, v1_ref), h, dv)
        o = _head_blocks((o0_ref, o1_ref), h, dv).astype(F32)
        i_col, i_row, fcum_col, fcum_row, ftot = _mlstm_gates(zg_ref, bias_ref, h, same, causal, eye)
        m_prev = m_scr[h:h + 1, 0:1]
        dlog = jnp.where(causal, fcum_col - fcum_row + i_row, -jnp.inf)
        m_inter = m_prev + fcum_col
        m_tok = jnp.maximum(m_inter, jnp.max(dlog, axis=1, keepdims=True))
        qb = q.astype(BF16)
        kb = k.astype(BF16)
        vb = v.astype(BF16)
        s = _dot_nt(qb, kb) * jnp.exp(dlog - m_tok)
        w_inter = jnp.exp(m_inter - m_tok)
        cmat = c_scr[h]
        nvec = n_scr[h:h + 1, :]
        num = _dot(s.astype(BF16), vb) + w_inter * _dot(qb, cmat.astype(BF16))
        den = jnp.sum(s, axis=1, keepdims=True) + w_inter * jnp.sum(q * nvec, axis=1, keepdims=True)
        den = jnp.maximum(jnp.abs(den), jnp.exp(-m_tok))
        hh = num / den
        ms = jnp.mean(hh * hh, axis=-1, keepdims=True)
        yn = (hh * lax.rsqrt(ms + EPS)) * gm_ref[:, h * dv:(h + 1) * dv]
        yb_ref[:, h * dv:(h + 1) * dv] = (yn * jax.nn.sigmoid(o)).astype(yb_ref.dtype)
        wlog = ftot - fcum_col + i_col
        m_new = jnp.maximum(m_prev + ftot[0:1, :], jnp.max(wlog, axis=0, keepdims=True))
        wk = jnp.exp(wlog - m_new)
        decay = jnp.exp(m_prev + ftot[0:1, :] - m_new)
        kw = k * wk
        c_scr[h] = decay * cmat + lax.dot_general(kw.astype(BF16), vb, (((0,), (0,)), ((), ())),
                                                  preferred_element_type=F32)
        n_scr[h:h + 1, :] = decay * nvec + jnp.sum(kw, axis=0, keepdims=True)
        m_scr[h:h + 1, :] = jnp.broadcast_to(m_new, (1, m_scr.shape[1]))

    @pl.when(step == pl.num_programs(1) - 1)
    def _():
        c_ref[0] = c_scr[...]
        n_ref[0] = n_scr[...]
        m_ref[0] = m_scr[0:MLSTM_HEADS, 0:1]


def _mlstm_prompt(z, zg, bias, g_m, batch, seq, q_blk, layer):
    m = z.shape[0]
    wb = g_m.shape[2]
    qw = wb // 2
    dk = qw // MLSTM_HEADS
    dv = wb // MLSTM_HEADS
    L = _pick(seq, (256, 128, 64, 32, 16, 8))
    nc = seq // L
    row = lambda b, c: b * nc + c
    zspec = lambda blk: pl.BlockSpec((L, qw), lambda b, c: (row(b, c), blk))
    est = 2 * 6 * L * qw * 4 + 2 * L * GATE_LANES * 4 + 2 * L * wb * 2 + 3 * MLSTM_HEADS * dk * dv * 4 + 16 * L * L * 4
    return pl.pallas_call(
        _mlstm_prompt_kernel,
        grid=(batch, nc),
        in_specs=[zspec(q_blk + i) for i in range(6)] + [
            pl.BlockSpec((L, GATE_LANES), lambda b, c: (row(b, c), 0)),
            _layer_spec(bias, layer),
            _layer_spec(g_m, layer),
        ],
        out_specs=[
            pl.BlockSpec((L, wb), lambda b, c: (row(b, c), 0)),
            pl.BlockSpec((1, MLSTM_HEADS, dk, dv), lambda b, c: (b, 0, 0, 0)),
            pl.BlockSpec((1, MLSTM_HEADS, dk), lambda b, c: (b, 0, 0)),
            pl.BlockSpec((1, MLSTM_HEADS, 1), lambda b, c: (b, 0, 0)),
        ],
        out_shape=[
            jax.ShapeDtypeStruct((m, wb), BF16),
            jax.ShapeDtypeStruct((batch, MLSTM_HEADS, dk, dv), F32),
            jax.ShapeDtypeStruct((batch, MLSTM_HEADS, dk), F32),
            jax.ShapeDtypeStruct((batch, MLSTM_HEADS, 1), F32),
        ],
        scratch_shapes=[
            pltpu.VMEM((MLSTM_HEADS, dk, dv), F32),
            pltpu.VMEM((MLSTM_HEADS, dk), F32),
            pltpu.VMEM((8, 128), F32),
        ],
        compiler_params=_params(("parallel", "arbitrary"), est),
    )(z, z, z, z, z, z, zg, bias, g_m)


def _mlstm_sample_kernel(q_ref, k_ref, v0_ref, v1_ref, o0_ref, o1_ref, zg_ref, bias_ref, gm_ref,
                         c0_ref, nrow_ref, mrow_ref, n0_ref, *rest, seq_rows):
    yb_ref, c_ref, n_ref, m_ref = rest[-4:]
    R = q_ref.shape[0]
    nb = R // seq_rows
    dk = q_ref.shape[1] // MLSTM_HEADS
    dv = 2 * v0_ref.shape[1] // MLSTM_HEADS
    ri = lax.broadcasted_iota(jnp.int32, (R, R), 0)
    ci = lax.broadcasted_iota(jnp.int32, (R, R), 1)
    same = _seq_of(ri, seq_rows) == _seq_of(ci, seq_rows)
    causal = same & (ci <= ri)
    eye = ci == ri
    rid = lax.broadcasted_iota(jnp.int32, (R, 1), 0)
    for h in range(MLSTM_HEADS):
        q = q_ref[:, h * dk:(h + 1) * dk].astype(F32) * (dk ** -0.5)
        k = k_ref[:, h * dk:(h + 1) * dk]
        v = _head_blocks((v0_ref, v1_ref), h, dv)
        o = _head_blocks((o0_ref, o1_ref), h, dv).astype(F32)
        i_col, i_row, fcum_col, fcum_row, ftot = _mlstm_gates(zg_ref, bias_ref, h, same, causal, eye)
        m_prev = mrow_ref[:, h:h + 1]
        dlog = jnp.where(causal, fcum_col - fcum_row + i_row, -jnp.inf)
        m_inter = m_prev + fcum_col
        m_tok = jnp.maximum(m_inter, jnp.max(dlog, axis=1, keepdims=True))
        qb = q.astype(BF16)
        kb = k.astype(BF16)
        vb = v.astype(BF16)
        s = _dot_nt(qb, kb) * jnp.exp(dlog - m_tok)
        w_inter = jnp.exp(m_inter - m_tok)
        inter = jnp.zeros((R, dv), F32)
        for b in range(nb):
            mine = _seq_of(rid, seq_rows) == b
            inter = inter + jnp.where(mine, _dot(qb, c0_ref[b, h].astype(BF16)), 0.0)
        num = _dot(s.astype(BF16), vb) + w_inter * inter
        nrow = nrow_ref[:, h * dk:(h + 1) * dk]
        den = jnp.sum(s, axis=1, keepdims=True) + w_inter * jnp.sum(q * nrow, axis=1, keepdims=True)
        den = jnp.maximum(jnp.abs(den), jnp.exp(-m_tok))
        hh = num / den
        ms = jnp.mean(hh * hh, axis=-1, keepdims=True)
        yn = (hh * lax.rsqrt(ms + EPS)) * gm_ref[:, h * dv:(h + 1) * dv]
        yb_ref[:, h * dv:(h + 1) * dv] = (yn * jax.nn.sigmoid(o)).astype(yb_ref.dtype)
        wlog_col = ftot - fcum_col + i_col
        wlog_row = _col_to_row(wlog_col, eye)
        m_new = jnp.maximum(m_prev + ftot, jnp.max(jnp.where(same, wlog_row, -jnp.inf), axis=1, keepdims=True))
        wk = jnp.exp(wlog_col - m_new)
        decay = jnp.exp(m_prev + ftot - m_new)
        kw = k * wk
        for b in range(nb):
            mine = _seq_of(rid, seq_rows) == b
            kw_b = jnp.where(mine, kw, 0.0)
            r0 = b * seq_rows
            dec_b = decay[r0:r0 + 1, :]
            upd = lax.dot_general(kw_b.astype(BF16), vb, (((0,), (0,)), ((), ())), preferred_element_type=F32)
            c_ref[b, h] = dec_b * c0_ref[b, h] + upd
            n_ref[b, h:h + 1, :] = dec_b * n0_ref[b, h:h + 1, :] + jnp.sum(kw_b, axis=0, keepdims=True)
            m_ref[0, b:b + 1, h:h + 1] = m_new[r0:r0 + 1, :]


def _mlstm_sample(z, zg, bias, g_m, c0, n0, n_rows, m_rows, c_acc, seq_rows, q_blk, layer):
    m = z.shape[0]
    depth, batch = c0.shape[:2]
    wb = g_m.shape[2]
    qw = wb // 2
    dk = qw // MLSTM_HEADS
    dv = wb // MLSTM_HEADS
    nb = 4
    R = nb * seq_rows
    zspec = lambda blk: pl.BlockSpec((R, qw), lambda s: (s, blk))
    cblk = nb * MLSTM_HEADS * dk * dv * 4
    est = 4 * cblk + 2 * 7 * R * qw * 4 + 2 * R * GATE_LANES * 4 + 2 * R * wb * 2
    in_specs = [zspec(q_blk + i) for i in range(6)] + [
        pl.BlockSpec((R, GATE_LANES), lambda s: (s, 0)),
        _layer_spec(bias, layer),
        _layer_spec(g_m, layer),
        pl.BlockSpec((None, nb, MLSTM_HEADS, dk, dv), lambda s: (layer, s, 0, 0, 0)),
        pl.BlockSpec((None, R, qw), lambda s: (layer, s, 0)),
        pl.BlockSpec((None, R, MLSTM_HEADS), lambda s: (layer, s, 0)),
        pl.BlockSpec((None, nb, MLSTM_HEADS, dk), lambda s: (layer, s, 0, 0)),
    ]
    args = [z, z, z, z, z, z, zg, bias, g_m, c0, n_rows, m_rows, n0]
    aliases = {}
    if c_acc is not None:
        in_specs.append(pl.BlockSpec(memory_space=pl.ANY))
        aliases = {len(args): 1}
        args.append(c_acc)
    yb, c_all, n1, m1 = pl.pallas_call(
        functools.partial(_mlstm_sample_kernel, seq_rows=seq_rows),
        grid=(m // R,),
        in_specs=in_specs,
        out_specs=[
            pl.BlockSpec((R, wb), lambda s: (s, 0)),
            pl.BlockSpec((None, nb, MLSTM_HEADS, dk, dv), lambda s: (layer, s, 0, 0, 0)),
            pl.BlockSpec((nb, MLSTM_HEADS, dk), lambda s: (s, 0, 0)),
            pl.BlockSpec((1, nb, MLSTM_HEADS), lambda s: (s, 0, 0)),
        ],
        out_shape=[
            jax.ShapeDtypeStruct((m, wb), BF16),
            jax.ShapeDtypeStruct(c0.shape, F32),
            jax.ShapeDtypeStruct(n0.shape[1:], F32),
            jax.ShapeDtypeStruct((batch // nb, nb, MLSTM_HEADS), F32),
        ],
        input_output_aliases=aliases,
        compiler_params=_params(("parallel",), est),
    )(*args)
    return yb, c_all, n1, m1.reshape(batch, MLSTM_HEADS)


def kernel(x_prompt, x_sample, state_pool, state_mlstm_c, state_mlstm_n, state_mlstm_m, p_prompt, p_sample, g_mix, w_in, b_igate, b_fgate, w_pool, pool_scale, g_mlstm, w_s, b_s, g_v, w_out, g_ffn, w_ffn_gate, w_ffn_up, w_ffn_down, g_ple, w_ple_gate, w_ple_proj, g_final):
    batch, seq, d = x_prompt.shape
    dec_batch, dec_seq, _ = x_sample.shape
    depth = w_in.shape[0]
    wa = pool_scale.shape[1]
    wb = g_mlstm.shape[1]
    wc = g_v.shape[1]
    n_gates = 2 * MLSTM_HEADS
    off_gate = wa + 2 * (wb // 2) + 2 * wb
    off_c = off_gate + n_gates
    q_blk = 0
    u_blk = (off_gate - wa) // wc
    assert seq % CHUNK == 0 and (off_gate - wa) % wc == 0

    w_in_t = jnp.swapaxes(w_in, 1, 2)
    wt_gate = jnp.pad(w_in_t[:, off_gate:off_c], ((0, 0), (0, GATE_LANES - n_gates), (0, 0)))
    gate_bias = jnp.pad(jnp.concatenate([b_igate, b_fgate], axis=1), ((0, 0), (0, GATE_LANES - n_gates)))[:, None, :]
    w_pool_b = w_pool.astype(BF16)
    w_fd = w_ffn_down.astype(BF16)
    reps = CHUNK // dec_seq
    ws_sample = jnp.tile(w_s[:, :, :dec_seq, :dec_seq], (1, 1, reps, reps))
    bias_prompt = jnp.swapaxes(b_s, 1, 2)
    bias_sample = jnp.tile(jnp.swapaxes(b_s[:, :, :dec_seq], 1, 2), (1, reps, 1))
    vec = lambda a: a[:, None, :]
    g_m3, g_v3, scale3, g_mix3, g_ffn3, g_ple3 = map(vec, (g_mlstm, g_v, pool_scale, g_mix, g_ffn, g_ple))
    g_fin3 = g_final[None, None, :]

    def in_projection(xb, ss, i):
        za = _inproj(xb, ss, w_in_t, 0, wa, F32, i)
        zr, zg = _inproj(xb, ss, w_in_t, wa, w_in_t.shape[1] - wa - n_gates, BF16, i,
                         skip_at=off_gate - wa, n_skip=n_gates, wt_gate=wt_gate)
        return za, zr, zg

    def next_mix(i):
        return (g_mix3, i + 1) if i + 1 < depth else (g_fin3, 0)

    pp = p_prompt.reshape(depth, batch * seq, -1)
    ps = p_sample.reshape(depth, dec_batch * dec_seq, -1)
    hist_tm = jnp.swapaxes(state_pool, 1, 2)
    n_rows = jnp.repeat(state_mlstm_n.reshape(depth, dec_batch, -1), dec_seq, axis=1)
    m_rows = jnp.repeat(state_mlstm_m, dec_seq, axis=1)

    xp = x_prompt.reshape(batch * seq, d)
    xs = x_sample.reshape(dec_batch * dec_seq, d)
    xpb, ssp = _prep_rows(xp, g_mix3, 0)
    xsb, sss = _prep_rows(xs, g_mix3, 0)
    v_start = ((seq - 1) // CHUNK) * CHUNK
    outs = {k: [] for k in ("pool_p", "pool_s", "c_p", "n_p", "n_s", "m_p", "m_s", "v_p", "v_s")}
    c_s = None

    for i in range(depth):
        zap, zp, zgp = in_projection(xpb, ssp, i)
        ya = _pool_prompt(zap, seq, w_pool_b, scale3, i)
        yb, c1, n1, m1 = _mlstm_prompt(zp, zgp, gate_bias, g_m3, batch, seq, q_blk, i)
        yc, vn = _chunk_mlp(zp, u_blk, w_s, bias_prompt, g_v3, seq, i)
        xp, xpb, ssp = _wout(ya, yb, yc, w_out, xp, i, g_ffn3, i)
        xp, xpb, ssp = _ffn_down(_ffn_up(xpb, ssp, w_ffn_gate, w_ffn_up, i), w_fd, xp, i, g_ple3, i)
        xp, xpb, ssp = _ple(xpb, ssp, w_ple_gate, pp, w_ple_proj, xp, i, *next_mix(i))
        outs["pool_p"].append(zap.reshape(batch, seq, wa)[:, seq - POOL_BUF:])
        outs["c_p"].append(c1)
        outs["n_p"].append(n1)
        outs["m_p"].append(m1.reshape(batch, MLSTM_HEADS))
        outs["v_p"].append(vn.reshape(batch, seq - v_start, wc))
        zas, zs, zgs = in_projection(xsb, sss, i)
        new_tm = jnp.swapaxes(zas.reshape(dec_batch, dec_seq, wa), 0, 1)
        ya_tm = _pool_sample(hist_tm, new_tm, w_pool_b, scale3, PAST_LEN, i)
        ya = jnp.swapaxes(ya_tm, 0, 1).reshape(dec_batch * dec_seq, wa)
        yb, c_s, n1, m1 = _mlstm_sample(zs, zgs, gate_bias, g_m3, state_mlstm_c, state_mlstm_n, n_rows, m_rows,
                                        c_s, dec_seq, q_blk, i)
        yc, vn = _chunk_mlp(zs, u_blk, ws_sample, bias_sample, g_v3, dec_seq, i)
        xs, xsb, sss = _wout(ya, yb, yc, w_out, xs, i, g_ffn3, i)
        xs, xsb, sss = _ffn_down(_ffn_up(xsb, sss, w_ffn_gate, w_ffn_up, i), w_fd, xs, i, g_ple3, i)
        xs, xsb, sss = _ple(xsb, sss, w_ple_gate, ps, w_ple_proj, xs, i, *next_mix(i))
        outs["pool_s"].append(jnp.concatenate([hist_tm[i], new_tm], axis=0)[dec_seq:])
        outs["n_s"].append(n1)
        outs["m_s"].append(m1)
        outs["v_s"].append(vn.reshape(dec_batch, dec_seq, wc))

    y_prompt = _final_norm(xp, g_final[None, :]).reshape(batch, seq, d)
    y_sample = _final_norm(xs, g_final[None, :]).reshape(dec_batch, dec_seq, d)
    st = lambda k: jnp.stack(outs[k])
    pool_sample = jnp.swapaxes(st("pool_s"), 1, 2)
    return (y_prompt, y_sample, st("pool_p"), pool_sample, st("c_p"), c_s, st("n_p"), st("n_s"),
            st("m_p"), st("m_s"), st("v_p"), st("v_s"))
```

```python
import functools
import math

import jax
import jax.numpy as jnp
from jax import lax
from jax.experimental import pallas as pl
from jax.experimental.pallas import tpu as pltpu

F32 = jnp.float32
BF16 = jnp.bfloat16

EPS = 1e-6
PAST_LEN = 16384
POOL_WINDOWS = (2, 4, 8, 16)
POOL_BUF = max(POOL_WINDOWS) - 1
POOL_HALO = 16
N_POOL = len(POOL_WINDOWS)
MLSTM_HEADS = 4
N_CM = 4
CHUNK = 128
GATE_LANES = 128
SS_LANES = 128

VMEM_SLACK_BYTES = 10 * 2**20
VMEM_CAP_BYTES = 58 * 2**20

_ROW_TILES = (1024, 512, 256, 128, 64, 32, 16, 8)
_TALL_ROW_TILES = (2048,) + _ROW_TILES
_STREAM_ROWS = (512, 256, 128, 64, 32, 16, 8)


def _params(sem, vmem_estimate):
    limit = min(int(vmem_estimate) + VMEM_SLACK_BYTES, VMEM_CAP_BYTES)
    return pltpu.CompilerParams(dimension_semantics=sem, vmem_limit_bytes=limit)


def _pick(n, candidates):
    for c in candidates:
        if n % c == 0:
            return c
    return n


def _single(shape, index_map):
    return pl.BlockSpec(shape, index_map, pipeline_mode=pl.Buffered(1))


def _layer_spec(arr, layer):
    nd = arr.ndim - 1
    return pl.BlockSpec((None,) + arr.shape[1:], lambda *_: (layer,) + (0,) * nd)


def _dot(a, b):
    return jnp.dot(a, b, preferred_element_type=F32)


def _dot_nt(a, bt):
    return lax.dot_general(a, bt, (((1,), (1,)), ((), ())), preferred_element_type=F32)


def _gelu_tanh(x):
    return 0.5 * x * (1.0 + jnp.tanh(math.sqrt(2.0 / math.pi) * (x + 0.044715 * (x * x * x))))


def _seq_of(idx, seq_rows):
    if seq_rows & (seq_rows - 1) == 0:
        return lax.shift_right_logical(idx, seq_rows.bit_length() - 1)
    return idx // seq_rows


def _log_sigmoid(x):
    return jnp.minimum(x, 0.0) - jnp.log(1.0 + jnp.exp(-jnp.abs(x)))


def _row_scale(ss_ref, width):
    return lax.rsqrt(ss_ref[:, 0:1] * (1.0 / width) + EPS)


def _gain_spec(g, g_idx):
    return pl.BlockSpec((None,) + g.shape[1:], lambda i, j: (g_idx, 0, 0))


MXU_COLS = 256


def _col_groups(bn):
    w = MXU_COLS if bn % MXU_COLS == 0 else bn
    return [slice(c, c + w) for c in range(0, bn, w)]


def _group_specs(n_rows, bn, layer):
    groups = _col_groups(bn)
    gw = bn // len(groups)
    return [pl.BlockSpec((None, n_rows, gw), lambda i, j, g=g: (layer, 0, j * len(groups) + g))
            for g in range(len(groups))]


def _emit_rows(new_cols, j, gn_ref, o_ref, xb_ref, ss_ref):
    bn = o_ref.shape[1]
    sq = None
    for grp, cols in enumerate(_col_groups(bn)):
        xn = new_cols(grp, cols)
        o_ref[:, cols] = xn
        g = gn_ref[:, pl.ds(pl.multiple_of(j * bn + cols.start, cols.stop - cols.start), cols.stop - cols.start)]
        xb_ref[:, cols] = (xn * g).astype(xb_ref.dtype)
        s = jnp.sum(xn * xn, axis=-1, keepdims=True)
        sq = s if sq is None else sq + s
    part = jnp.broadcast_to(sq, ss_ref.shape)

    @pl.when(j == 0)
    def _():
        ss_ref[...] = part

    @pl.when(j > 0)
    def _():
        ss_ref[...] += part


def _row_outs(m, d, bm, bn):
    specs = [
        pl.BlockSpec((bm, bn), lambda i, j: (i, j)),
        pl.BlockSpec((bm, bn), lambda i, j: (i, j)),
        pl.BlockSpec((bm, SS_LANES), lambda i, j: (i, 0)),
    ]
    shapes = [jax.ShapeDtypeStruct((m, d), F32), jax.ShapeDtypeStruct((m, d), BF16),
              jax.ShapeDtypeStruct((m, SS_LANES), F32)]
    return specs, shapes


def _prep_rows_kernel(x_ref, gn_ref, xb_ref, ss_ref):
    x = x_ref[...]
    xb_ref[...] = (x * gn_ref[...]).astype(xb_ref.dtype)
    ss_ref[...] = jnp.broadcast_to(jnp.sum(x * x, axis=-1, keepdims=True), ss_ref.shape)


def _prep_rows(x, g, g_idx):
    m, d = x.shape
    bm = _pick(m, _STREAM_ROWS)
    return pl.pallas_call(
        _prep_rows_kernel,
        grid=(m // bm,),
        in_specs=[pl.BlockSpec((bm, d), lambda i: (i, 0)), _layer_spec(g, g_idx)],
        out_specs=[pl.BlockSpec((bm, d), lambda i: (i, 0)), pl.BlockSpec((bm, SS_LANES), lambda i: (i, 0))],
        out_shape=[jax.ShapeDtypeStruct((m, d), BF16), jax.ShapeDtypeStruct((m, SS_LANES), F32)],
        compiler_params=_params(("parallel",), 2 * bm * d * 6),
    )(x, g)


def _inproj_kernel(xb_ref, ss_ref, *refs):
    wgt_ref, z_ref, zg_ref = refs[-3:]
    wt_refs = refs[:-3]
    rs = _row_scale(ss_ref, xb_ref.shape[1])

    @pl.when(pl.program_id(1) == 0)
    def _():
        zg_ref[...] = rs * _dot_nt(xb_ref[...], wgt_ref[...].astype(BF16))

    for grp, cols in enumerate(_col_groups(z_ref.shape[1])):
        z_ref[:, cols] = rs * _dot_nt(xb_ref[...], wt_refs[grp][0].astype(BF16))


def _inproj(xb, ss, wt, wt_gate, n_lead, n_skip, layer):
    m, d = xb.shape
    n = wt.shape[1] - n_skip
    bm = _pick(m, _TALL_ROW_TILES)
    bn = _pick(math.gcd(n_lead, n), (512, 256, 128))
    lead_blocks = n_lead // bn
    assert n_skip % 8 == 0 and n_lead % bn == 0 and n % bn == 0
    est = (bm * d * 2 + 2 * d * bn * 4 + 2 * d * GATE_LANES * 4 + 2 * bm * bn * 4 + 4 * bm * GATE_LANES * 4
           + d * bn * 2)
    groups = _col_groups(bn)
    gw = bn // len(groups)

    def rows_of(grp):
        def index(i, j):
            row = j * bn + grp * gw + jnp.where(j >= lead_blocks, n_skip, 0)
            return (layer, pl.multiple_of(row, 8), 0)
        return pl.BlockSpec((pl.Element(1), pl.Element(gw), pl.Element(d)), index)

    return pl.pallas_call(
        _inproj_kernel,
        grid=(m // bm, n // bn),
        in_specs=[
            _single((bm, d), lambda i, j: (i, 0)),
            pl.BlockSpec((bm, SS_LANES), lambda i, j: (i, 0)),
            *[rows_of(grp) for grp in range(len(groups))],
            _layer_spec(wt_gate, layer),
        ],
        out_specs=[
            pl.BlockSpec((bm, bn), lambda i, j: (i, j)),
            pl.BlockSpec((bm, GATE_LANES), lambda i, j: (i, 0)),
        ],
        out_shape=[jax.ShapeDtypeStruct((m, n), F32), jax.ShapeDtypeStruct((m, GATE_LANES), F32)],
        compiler_params=_params(("parallel", "arbitrary"), est),
    )(xb, ss, *[wt] * len(groups), wt_gate)


def _ffn_up_kernel(xb_ref, ss_ref, wg_ref, wu_ref, o_ref):
    rs = _row_scale(ss_ref, xb_ref.shape[1])
    h = xb_ref[...]
    a = rs * _dot(h, wg_ref[...].astype(BF16))
    b = rs * _dot(h, wu_ref[...].astype(BF16))
    o_ref[...] = ((a * jax.nn.sigmoid(a)) * b).astype(o_ref.dtype)


def _ffn_up(xb, ss, w_gate, w_up, layer):
    m, d = xb.shape
    n = w_gate.shape[2]
    bm = _pick(m, _TALL_ROW_TILES)
    tb = _pick(n, (256, 128))
    est = bm * d * 2 + 4 * d * tb * 4 + 2 * d * tb * 2 + 2 * bm * tb * 2 + 2 * bm * SS_LANES * 4
    wspec = pl.BlockSpec((None, d, tb), lambda i, j: (layer, 0, j))
    return pl.pallas_call(
        _ffn_up_kernel,
        grid=(m // bm, n // tb),
        in_specs=[_single((bm, d), lambda i, j: (i, 0)), pl.BlockSpec((bm, SS_LANES), lambda i, j: (i, 0)),
                  wspec, wspec],
        out_specs=pl.BlockSpec((bm, tb), lambda i, j: (i, j)),
        out_shape=jax.ShapeDtypeStruct((m, n), BF16),
        compiler_params=_params(("parallel", "arbitrary"), est),
    )(xb, ss, w_gate, w_up)


def _ple_kernel(xb_ref, ss_ref, *refs):
    p_ref, wpp_ref, x_ref, gn_ref, o_ref, xbo_ref, sso_ref = refs[-7:]
    w_refs = refs[:-7]
    rs = _row_scale(ss_ref, xb_ref.shape[1])
    pb = p_ref[...].astype(BF16)

    def new_cols(grp, cols):
        gate = jax.nn.sigmoid(rs * _dot(xb_ref[...], w_refs[grp][...].astype(BF16)))
        emb = _dot(pb, wpp_ref[:, cols].astype(BF16))
        return x_ref[:, cols] + gate * emb

    _emit_rows(new_cols, pl.program_id(1), gn_ref, o_ref, xbo_ref, sso_ref)


def _ple(xb, ss, w_pg, p, w_pp, x, layer, g_next, g_idx):
    m, d = x.shape
    pd = p.shape[2]
    bm = _pick(m, _ROW_TILES)
    bn = _pick(d, (512, 256, 128))
    est = (2 * bm * d * 2 + 2 * d * bn * 4 + d * bn * 2 + 2 * bm * pd * 4 + 2 * pd * bn * 4 + 4 * bm * bn * 4
           + 2 * bm * bn * 2 + 4 * bm * SS_LANES * 4)
    out_specs, out_shape = _row_outs(m, d, bm, bn)
    w_specs = _group_specs(d, bn, layer)
    return pl.pallas_call(
        _ple_kernel,
        grid=(m // bm, d // bn),
        in_specs=[
            pl.BlockSpec((bm, d), lambda i, j: (i, 0)),
            pl.BlockSpec((bm, SS_LANES), lambda i, j: (i, 0)),
            *w_specs,
            pl.BlockSpec((None, bm, pd), lambda i, j: (layer, i, 0)),
            pl.BlockSpec((None, pd, bn), lambda i, j: (layer, 0, j)),
            pl.BlockSpec((bm, bn), lambda i, j: (i, j)),
            _gain_spec(g_next, g_idx),
        ],
        out_specs=out_specs,
        out_shape=out_shape,
        compiler_params=_params(("parallel", "arbitrary"), est),
    )(xb, ss, *[w_pg] * len(w_specs), p, w_pp, x, g_next)


def _wout_kernel(ya_ref, yb_ref, yc_ref, *refs):
    x_ref, gn_ref, o_ref, xbo_ref, sso_ref = refs[-5:]
    w_refs = refs[:-5]
    wa = ya_ref.shape[1]
    wb = yb_ref.shape[1]
    ya = ya_ref[...].astype(BF16)

    def new_cols(grp, cols):
        w_ref = w_refs[grp]
        xn = x_ref[:, cols] + _dot(ya, w_ref[0:wa, :].astype(BF16))
        xn = xn + _dot(yb_ref[...], w_ref[wa:wa + wb, :].astype(BF16))
        return xn + _dot(yc_ref[...], w_ref[wa + wb:, :].astype(BF16))

    _emit_rows(new_cols, pl.program_id(1), gn_ref, o_ref, xbo_ref, sso_ref)


def _wout(ya, yb, yc, w, x, layer, g_next, g_idx):
    m, d = x.shape
    wa, wb, wc = ya.shape[1], yb.shape[1], yc.shape[1]
    k = wa + wb + wc
    bm = _pick(m, _ROW_TILES)
    bn = _pick(d, (512, 256, 128))
    est = (2 * bm * wa * ya.dtype.itemsize + 2 * bm * wb * 2 + 2 * bm * wc * 2
           + 2 * k * bn * 4 + k * bn * 2 + 4 * bm * bn * 4 + 2 * bm * bn * 2 + 2 * bm * SS_LANES * 4)
    out_specs, out_shape = _row_outs(m, d, bm, bn)
    w_specs = _group_specs(k, bn, layer)
    return pl.pallas_call(
        _wout_kernel,
        grid=(m // bm, d // bn),
        in_specs=[
            pl.BlockSpec((bm, wa), lambda i, j: (i, 0)),
            pl.BlockSpec((bm, wb), lambda i, j: (i, 0)),
            pl.BlockSpec((bm, wc), lambda i, j: (i, 0)),
            *w_specs,
            pl.BlockSpec((bm, bn), lambda i, j: (i, j)),
            _gain_spec(g_next, g_idx),
        ],
        out_specs=out_specs,
        out_shape=out_shape,
        compiler_params=_params(("parallel", "arbitrary"), est),
    )(ya, yb, yc, *[w] * len(w_specs), x, g_next)


def _ffn_down_kernel(a_ref, *refs):
    x_ref, gn_ref, o_ref, xbo_ref, sso_ref = refs[-5:]
    w_refs = refs[:-5]
    _emit_rows(lambda grp, cols: x_ref[:, cols] + _dot(a_ref[...], w_refs[grp][...]),
               pl.program_id(1), gn_ref, o_ref, xbo_ref, sso_ref)


def _ffn_down(a, w, x, layer, g_next, g_idx):
    m, d = x.shape
    k = a.shape[1]
    bm = _pick(m, _ROW_TILES[1:])
    bn = _pick(d, (512, 256, 128))
    est = 2 * bm * k * 2 + 2 * k * bn * 2 + 4 * bm * bn * 4 + 2 * bm * bn * 2 + 2 * bm * SS_LANES * 4
    out_specs, out_shape = _row_outs(m, d, bm, bn)
    w_specs = _group_specs(k, bn, layer)
    return pl.pallas_call(
        _ffn_down_kernel,
        grid=(m // bm, d // bn),
        in_specs=[
            pl.BlockSpec((bm, k), lambda i, j: (i, 0)),
            *w_specs,
            pl.BlockSpec((bm, bn), lambda i, j: (i, j)),
            _gain_spec(g_next, g_idx),
        ],
        out_specs=out_specs,
        out_shape=out_shape,
        compiler_params=_params(("parallel", "arbitrary"), est),
    )(a, *[w] * len(w_specs), x, g_next)


def _final_norm_kernel(x_ref, g_ref, o_ref):
    x = x_ref[...]
    ms = jnp.mean(x * x, axis=-1, keepdims=True)
    o_ref[...] = (x * lax.rsqrt(ms + EPS)) * g_ref[...]


def _final_norm(x, g):
    m, d = x.shape
    bm = _pick(m, _STREAM_ROWS)
    return pl.pallas_call(
        _final_norm_kernel,
        grid=(m // bm,),
        in_specs=[pl.BlockSpec((bm, d), lambda i: (i, 0)), pl.BlockSpec((1, d), lambda i: (0, 0))],
        out_specs=pl.BlockSpec((bm, d), lambda i: (i, 0)),
        out_shape=jax.ShapeDtypeStruct((m, d), F32),
        compiler_params=_params(("parallel",), 4 * bm * d * 4),
    )(x, g)


def _pool_prompt_kernel(a_ref, halo_ref, wp_ref, sc_ref, y_ref, ext_scr, *, tiles_per_seq):
    tp = a_ref.shape[0]
    cg = wp_ref.shape[1]
    tile = pl.program_id(0) % tiles_per_seq
    ext_scr[0:POOL_HALO, :] = jnp.where(tile == 0, 0.0, halo_ref[...])
    ext_scr[POOL_HALO:, :] = a_ref[...]
    pos = tile * tp + lax.broadcasted_iota(jnp.int32, (tp, 1), 0)
    for g, w in enumerate(POOL_WINDOWS):
        sl = slice(g * cg, (g + 1) * cg)
        s = ext_scr[POOL_HALO:POOL_HALO + tp, sl]
        for k in range(1, w):
            s = s + ext_scr[POOL_HALO - k:POOL_HALO - k + tp, sl]
        cnt = jnp.minimum(pos + 1, w).astype(F32)
        dlt = s / cnt - a_ref[:, sl]
        y = _dot(dlt.astype(BF16), wp_ref[g]) * sc_ref[:, sl]
        y_ref[:, sl] = y.astype(y_ref.dtype)


def _pool_prompt(z, seq, w_pool, scale, layer):
    m = z.shape[0]
    wa = scale.shape[2]
    tp = _pick(seq, (512, 256, 128, 64, 32, 16))
    tiles_per_seq = seq // tp
    halo_blocks = tp // POOL_HALO
    est = (2 * tp * wa * 4 + 2 * POOL_HALO * wa * 4 + (tp + POOL_HALO) * wa * 4 + 2 * tp * wa * 2
           + 4 * wa * wa // N_POOL)
    return pl.pallas_call(
        functools.partial(_pool_prompt_kernel, tiles_per_seq=tiles_per_seq),
        grid=(m // tp,),
        in_specs=[
            pl.BlockSpec((tp, wa), lambda i: (i, 0)),
            pl.BlockSpec((POOL_HALO, wa), lambda i: (jnp.maximum(i * halo_blocks - 1, 0), 0)),
            _layer_spec(w_pool, layer),
            _layer_spec(scale, layer),
        ],
        out_specs=pl.BlockSpec((tp, wa), lambda i: (i, 0)),
        out_shape=jax.ShapeDtypeStruct((m, wa), BF16),
        scratch_shapes=[pltpu.VMEM((tp + POOL_HALO, wa), F32)],
        compiler_params=_params(("parallel",), est),
    )(z, z, w_pool, scale)


def _pool_sample_kernel(hist_ref, new_ref, wp_ref, sc_ref, y_ref, *, first_pos):
    cg = wp_ref.shape[1]
    n_new = new_ref.shape[0]
    rows = [hist_ref[r] for r in range(POOL_BUF)] + [new_ref[t] for t in range(n_new)]
    for t in range(n_new):
        for g, w in enumerate(POOL_WINDOWS):
            sl = slice(g * cg, (g + 1) * cg)
            s = rows[POOL_BUF + t][:, sl]
            for k in range(1, w):
                s = s + rows[POOL_BUF + t - k][:, sl]
            cnt = float(min(first_pos + t + 1, w))
            dlt = s / cnt - rows[POOL_BUF + t][:, sl]
            y = _dot(dlt.astype(BF16), wp_ref[g]) * sc_ref[:, sl]
            y_ref[t, :, sl] = y.astype(y_ref.dtype)


def _pool_sample(hist, new, w_pool, scale, first_pos, layer):
    n_new, b, wa = new.shape
    bb = _pick(b, (32, 16, 8))
    est = 2 * (POOL_BUF + 2 * n_new) * bb * wa * 4 + 4 * wa * wa // N_POOL
    return pl.pallas_call(
        functools.partial(_pool_sample_kernel, first_pos=first_pos),
        grid=(b // bb,),
        in_specs=[
            pl.BlockSpec((None, POOL_BUF, bb, wa), lambda i: (layer, 0, i, 0)),
            pl.BlockSpec((n_new, bb, wa), lambda i: (0, i, 0)),
            _layer_spec(w_pool, layer),
            _layer_spec(scale, layer),
        ],
        out_specs=pl.BlockSpec((n_new, bb, wa), lambda i: (0, i, 0)),
        out_shape=jax.ShapeDtypeStruct((n_new, b, wa), F32),
        compiler_params=_params(("parallel",), est),
    )(hist, new, w_pool, scale)


def _chunk_mlp_kernel(u_ref, v_ref, ws_ref, b_ref, gv_ref, yc_ref, vn_ref, *, seq_rows):
    rows = u_ref.shape[0]
    cg = u_ref.shape[1] // N_CM
    ri = lax.broadcasted_iota(jnp.int32, (rows, rows), 0)
    ci = lax.broadcasted_iota(jnp.int32, (rows, rows), 1)
    mask = ci <= ri
    if seq_rows < rows:
        mask = mask & (_seq_of(ri, seq_rows) == _seq_of(ci, seq_rows))
    for g in range(N_CM):
        sl = slice(g * cg, (g + 1) * cg)
        u = _gelu_tanh(u_ref[:, sl])
        v = _gelu_tanh(v_ref[:, sl])
        ms = jnp.mean(v * v, axis=-1, keepdims=True)
        vn = (v * lax.rsqrt(ms + EPS)) * gv_ref[:, sl]
        vn_ref[:, sl] = vn
        wm = jnp.where(mask, ws_ref[g], 0.0).astype(BF16)
        mix = _dot(wm, vn.astype(BF16)) + b_ref[:, g:g + 1]
        yc_ref[:, sl] = (u * mix).astype(yc_ref.dtype)


def _chunk_mlp(z, u_blk, w_mix, bias, g_v, seq_rows, layer):
    m = z.shape[0]
    wc = g_v.shape[2]
    tiles_per_seq = max(seq_rows // CHUNK, 1)
    seq_rows = min(seq_rows, CHUNK)
    assert m % (tiles_per_seq * CHUNK) == 0
    est = 4 * CHUNK * wc * 4 + 2 * N_CM * CHUNK * CHUNK * 4 + 2 * CHUNK * wc * 2 + 2 * CHUNK * wc * 4
    return pl.pallas_call(
        functools.partial(_chunk_mlp_kernel, seq_rows=seq_rows),
        grid=(m // CHUNK,),
        in_specs=[
            pl.BlockSpec((CHUNK, wc), lambda i: (i, u_blk)),
            pl.BlockSpec((CHUNK, wc), lambda i: (i, u_blk + 1)),
            _layer_spec(w_mix, layer),
            _layer_spec(bias, layer),
            _layer_spec(g_v, layer),
        ],
        out_specs=[pl.BlockSpec((CHUNK, wc), lambda i: (i, 0)),
                   pl.BlockSpec((CHUNK, wc), lambda i: (i // tiles_per_seq, 0))],
        out_shape=[jax.ShapeDtypeStruct((m, wc), BF16), jax.ShapeDtypeStruct((m // tiles_per_seq, wc), F32)],
        compiler_params=_params(("arbitrary",), est),
    )(z, z, w_mix, bias, g_v)


def _col_to_row(col, eye):
    return jnp.sum(jnp.where(eye, col, 0.0), axis=0, keepdims=True)


def _head_blocks(refs, h, width):
    r = refs[h // 2]
    return r[:, (h % 2) * width:(h % 2 + 1) * width]


def _mlstm_gates(zg_ref, bias_ref, h, same, causal, eye):
    i_col = zg_ref[:, h:h + 1] + bias_ref[:, h:h + 1]
    f_col = zg_ref[:, MLSTM_HEADS + h:MLSTM_HEADS + h + 1] + bias_ref[:, MLSTM_HEADS + h:MLSTM_HEADS + h + 1]
    lf_col = _log_sigmoid(f_col)
    lf_row = _col_to_row(lf_col, eye)
    i_row = _col_to_row(i_col, eye)
    fcum_col = jnp.sum(jnp.where(causal, lf_row, 0.0), axis=1, keepdims=True)
    fcum_row = _col_to_row(fcum_col, eye)
    ftot_col = jnp.sum(jnp.where(same, lf_row, 0.0), axis=1, keepdims=True)
    return i_col, i_row, fcum_col, fcum_row, ftot_col


def _mlstm_prompt_kernel(q_ref, k_ref, v0_ref, v1_ref, o0_ref, o1_ref, zg_ref, bias_ref, gm_ref,
                         yb_ref, c_ref, n_ref, m_ref, c_scr, n_scr, m_scr):
    L = q_ref.shape[0]
    dk = q_ref.shape[1] // MLSTM_HEADS
    dv = 2 * v0_ref.shape[1] // MLSTM_HEADS
    step = pl.program_id(1)

    @pl.when(step == 0)
    def _():
        c_scr[...] = jnp.zeros_like(c_scr)
        n_scr[...] = jnp.zeros_like(n_scr)
        m_scr[...] = jnp.zeros_like(m_scr)

    ri = lax.broadcasted_iota(jnp.int32, (L, L), 0)
    ci = lax.broadcasted_iota(jnp.int32, (L, L), 1)
    causal = ci <= ri
    eye = ci == ri
    same = ci >= 0
    for h in range(MLSTM_HEADS):
        q = q_ref[:, h * dk:(h + 1) * dk] * (dk ** -0.5)
        k = k_ref[:, h * dk:(h + 1) * dk]
        v = _head_blocks((v0_ref, v1_ref), h, dv)
        o = _head_blocks((o0_ref, o1_ref), h, dv)
        i_col, i_row, fcum_col, fcum_row, ftot = _mlstm_gates(zg_ref, bias_ref, h, same, causal, eye)
        m_prev = m_scr[h:h + 1, 0:1]
        dlog = jnp.where(causal, fcum_col - fcum_row + i_row, -jnp.inf)
        m_inter = m_prev + fcum_col
        m_tok = jnp.maximum(m_inter, jnp.max(dlog, axis=1, keepdims=True))
        qb = q.astype(BF16)
        kb = k.astype(BF16)
        vb = v.astype(BF16)
        s = _dot_nt(qb, kb) * jnp.exp(dlog - m_tok)
        w_inter = jnp.exp(m_inter - m_tok)
        cmat = c_scr[h]
        nvec = n_scr[h:h + 1, :]
        num = _dot(s.astype(BF16), vb) + w_inter * _dot(qb, cmat.astype(BF16))
        den = jnp.sum(s, axis=1, keepdims=True) + w_inter * jnp.sum(q * nvec, axis=1, keepdims=True)
        den = jnp.maximum(jnp.abs(den), jnp.exp(-m_tok))
        hh = num / den
        ms = jnp.mean(hh * hh, axis=-1, keepdims=True)
        yn = (hh * lax.rsqrt(ms + EPS)) * gm_ref[:, h * dv:(h + 1) * dv]
        yb_ref[:, h * dv:(h + 1) * dv] = (yn * jax.nn.sigmoid(o)).astype(yb_ref.dtype)
        wlog = ftot - fcum_col + i_col
        m_new = jnp.maximum(m_prev + ftot[0:1, :], jnp.max(wlog, axis=0, keepdims=True))
        wk = jnp.exp(wlog - m_new)
        decay = jnp.exp(m_prev + ftot[0:1, :] - m_new)
        kw = k * wk
        c_scr[h] = decay * cmat + lax.dot_general(kw.astype(BF16), vb, (((0,), (0,)), ((), ())),
                                                  preferred_element_type=F32)
        n_scr[h:h + 1, :] = decay * nvec + jnp.sum(kw, axis=0, keepdims=True)
        m_scr[h:h + 1, :] = jnp.broadcast_to(m_new, (1, m_scr.shape[1]))

    @pl.when(step == pl.num_programs(1) - 1)
    def _():
        c_ref[0] = c_scr[...]
        n_ref[0] = n_scr[...]
        m_ref[0] = m_scr[0:MLSTM_HEADS, 0:1]


def _mlstm_prompt(z, zg, bias, g_m, batch, seq, q_blk, layer):
    m = z.shape[0]
    wb = g_m.shape[2]
    qw = wb // 2
    dk = qw // MLSTM_HEADS
    dv = wb // MLSTM_HEADS
    L = _pick(seq, (256, 128, 64, 32, 16, 8))
    nc = seq // L
    row = lambda b, c: b * nc + c
    zspec = lambda blk: pl.BlockSpec((L, qw), lambda b, c: (row(b, c), blk))
    est = 2 * 6 * L * qw * 4 + 2 * L * GATE_LANES * 4 + 2 * L * wb * 2 + 3 * MLSTM_HEADS * dk * dv * 4 + 16 * L * L * 4
    return pl.pallas_call(
        _mlstm_prompt_kernel,
        grid=(batch, nc),
        in_specs=[zspec(q_blk + i) for i in range(6)] + [
            pl.BlockSpec((L, GATE_LANES), lambda b, c: (row(b, c), 0)),
            _layer_spec(bias, layer),
            _layer_spec(g_m, layer),
        ],
        out_specs=[
            pl.BlockSpec((L, wb), lambda b, c: (row(b, c), 0)),
            pl.BlockSpec((1, MLSTM_HEADS, dk, dv), lambda b, c: (b, 0, 0, 0)),
            pl.BlockSpec((1, MLSTM_HEADS, dk), lambda b, c: (b, 0, 0)),
            pl.BlockSpec((1, MLSTM_HEADS, 1), lambda b, c: (b, 0, 0)),
        ],
        out_shape=[
            jax.ShapeDtypeStruct((m, wb), BF16),
            jax.ShapeDtypeStruct((batch, MLSTM_HEADS, dk, dv), F32),
            jax.ShapeDtypeStruct((batch, MLSTM_HEADS, dk), F32),
            jax.ShapeDtypeStruct((batch, MLSTM_HEADS, 1), F32),
        ],
        scratch_shapes=[
            pltpu.VMEM((MLSTM_HEADS, dk, dv), F32),
            pltpu.VMEM((MLSTM_HEADS, dk), F32),
            pltpu.VMEM((8, 128), F32),
        ],
        compiler_params=_params(("parallel", "arbitrary"), est),
    )(z, z, z, z, z, z, zg, bias, g_m)


def _mlstm_sample_kernel(q_ref, k_ref, v0_ref, v1_ref, o0_ref, o1_ref, zg_ref, bias_ref, gm_ref,
                         c0_ref, nrow_ref, mrow_ref, n0_ref, *rest, seq_rows):
    yb_ref, c_ref, n_ref, m_ref = rest[-4:]
    R = q_ref.shape[0]
    nb = R // seq_rows
    dk = q_ref.shape[1] // MLSTM_HEADS
    dv = 2 * v0_ref.shape[1] // MLSTM_HEADS
    ri = lax.broadcasted_iota(jnp.int32, (R, R), 0)
    ci = lax.broadcasted_iota(jnp.int32, (R, R), 1)
    same = _seq_of(ri, seq_rows) == _seq_of(ci, seq_rows)
    causal = same & (ci <= ri)
    eye = ci == ri
    rid = lax.broadcasted_iota(jnp.int32, (R, 1), 0)
    for h in range(MLSTM_HEADS):
        q = q_ref[:, h * dk:(h + 1) * dk] * (dk ** -0.5)
        k = k_ref[:, h * dk:(h + 1) * dk]
        v = _head_blocks((v0_ref, v1_ref), h, dv)
        o = _head_blocks((o0_ref, o1_ref), h, dv)
        i_col, i_row, fcum_col, fcum_row, ftot = _mlstm_gates(zg_ref, bias_ref, h, same, causal, eye)
        m_prev = mrow_ref[:, h:h + 1]
        dlog = jnp.where(causal, fcum_col - fcum_row + i_row, -jnp.inf)
        m_inter = m_prev + fcum_col
        m_tok = jnp.maximum(m_inter, jnp.max(dlog, axis=1, keepdims=True))
        qb = q.astype(BF16)
        kb = k.astype(BF16)
        vb = v.astype(BF16)
        s = _dot_nt(qb, kb) * jnp.exp(dlog - m_tok)
        w_inter = jnp.exp(m_inter - m_tok)
        inter = jnp.zeros((R, dv), F32)
        for b in range(nb):
            mine = _seq_of(rid, seq_rows) == b
            inter = inter + jnp.where(mine, _dot(qb, c0_ref[b, h].astype(BF16)), 0.0)
        num = _dot(s.astype(BF16), vb) + w_inter * inter
        nrow = nrow_ref[:, h * dk:(h + 1) * dk]
        den = jnp.sum(s, axis=1, keepdims=True) + w_inter * jnp.sum(q * nrow, axis=1, keepdims=True)
        den = jnp.maximum(jnp.abs(den), jnp.exp(-m_tok))
        hh = num / den
        ms = jnp.mean(hh * hh, axis=-1, keepdims=True)
        yn = (hh * lax.rsqrt(ms + EPS)) * gm_ref[:, h * dv:(h + 1) * dv]
        yb_ref[:, h * dv:(h + 1) * dv] = (yn * jax.nn.sigmoid(o)).astype(yb_ref.dtype)
        wlog_col = ftot - fcum_col + i_col
        wlog_row = _col_to_row(wlog_col, eye)
        m_new = jnp.maximum(m_prev + ftot, jnp.max(jnp.where(same, wlog_row, -jnp.inf), axis=1, keepdims=True))
        wk = jnp.exp(wlog_col - m_new)
        decay = jnp.exp(m_prev + ftot - m_new)
        kw = k * wk
        for b in range(nb):
            mine = _seq_of(rid, seq_rows) == b
            kw_b = jnp.where(mine, kw, 0.0)
            r0 = b * seq_rows
            dec_b = decay[r0:r0 + 1, :]
            upd = lax.dot_general(kw_b.astype(BF16), vb, (((0,), (0,)), ((), ())), preferred_element_type=F32)
            c_ref[b, h] = dec_b * c0_ref[b, h] + upd
            n_ref[b, h:h + 1, :] = dec_b * n0_ref[b, h:h + 1, :] + jnp.sum(kw_b, axis=0, keepdims=True)
            m_ref[0, b:b + 1, h:h + 1] = m_new[r0:r0 + 1, :]


def _mlstm_sample(z, zg, bias, g_m, c0, n0, n_rows, m_rows, c_acc, seq_rows, q_blk, layer):
    m = z.shape[0]
    depth, batch = c0.shape[:2]
    wb = g_m.shape[2]
    qw = wb // 2
    dk = qw // MLSTM_HEADS
    dv = wb // MLSTM_HEADS
    nb = 4
    R = nb * seq_rows
    zspec = lambda blk: pl.BlockSpec((R, qw), lambda s: (s, blk))
    cblk = nb * MLSTM_HEADS * dk * dv * 4
    est = 4 * cblk + 2 * 7 * R * qw * 4 + 2 * R * GATE_LANES * 4 + 2 * R * wb * 2
    in_specs = [zspec(q_blk + i) for i in range(6)] + [
        pl.BlockSpec((R, GATE_LANES), lambda s: (s, 0)),
        _layer_spec(bias, layer),
        _layer_spec(g_m, layer),
        pl.BlockSpec((None, nb, MLSTM_HEADS, dk, dv), lambda s: (layer, s, 0, 0, 0)),
        pl.BlockSpec((None, R, qw), lambda s: (layer, s, 0)),
        pl.BlockSpec((None, R, MLSTM_HEADS), lambda s: (layer, s, 0)),
        pl.BlockSpec((None, nb, MLSTM_HEADS, dk), lambda s: (layer, s, 0, 0)),
    ]
    args = [z, z, z, z, z, z, zg, bias, g_m, c0, n_rows, m_rows, n0]
    aliases = {}
    if c_acc is not None:
        in_specs.append(pl.BlockSpec(memory_space=pl.ANY))
        aliases = {len(args): 1}
        args.append(c_acc)
    yb, c_all, n1, m1 = pl.pallas_call(
        functools.partial(_mlstm_sample_kernel, seq_rows=seq_rows),
        grid=(m // R,),
        in_specs=in_specs,
        out_specs=[
            pl.BlockSpec((R, wb), lambda s: (s, 0)),
            pl.BlockSpec((None, nb, MLSTM_HEADS, dk, dv), lambda s: (layer, s, 0, 0, 0)),
            pl.BlockSpec((nb, MLSTM_HEADS, dk), lambda s: (s, 0, 0)),
            pl.BlockSpec((1, nb, MLSTM_HEADS), lambda s: (s, 0, 0)),
        ],
        out_shape=[
            jax.ShapeDtypeStruct((m, wb), BF16),
            jax.ShapeDtypeStruct(c0.shape, F32),
            jax.ShapeDtypeStruct(n0.shape[1:], F32),
            jax.ShapeDtypeStruct((batch // nb, nb, MLSTM_HEADS), F32),
        ],
        input_output_aliases=aliases,
        compiler_params=_params(("parallel",), est),
    )(*args)
    return yb, c_all, n1, m1.reshape(batch, MLSTM_HEADS)


def kernel(x_prompt, x_sample, state_pool, state_mlstm_c, state_mlstm_n, state_mlstm_m, p_prompt, p_sample, g_mix, w_in, b_igate, b_fgate, w_pool, pool_scale, g_mlstm, w_s, b_s, g_v, w_out, g_ffn, w_ffn_gate, w_ffn_up, w_ffn_down, g_ple, w_ple_gate, w_ple_proj, g_final):
    batch, seq, d = x_prompt.shape
    dec_batch, dec_seq, _ = x_sample.shape
    depth = w_in.shape[0]
    wa = pool_scale.shape[1]
    wb = g_mlstm.shape[1]
    wc = g_v.shape[1]
    n_gates = 2 * MLSTM_HEADS
    off_gate = wa + 2 * (wb // 2) + 2 * wb
    off_c = off_gate + n_gates
    q_blk = wa // (wb // 2)
    u_blk = off_gate // wc
    assert seq % CHUNK == 0

    w_in_t = jnp.swapaxes(w_in, 1, 2)
    wt_gate = jnp.pad(w_in_t[:, off_gate:off_c], ((0, 0), (0, GATE_LANES - n_gates), (0, 0)))
    gate_bias = jnp.pad(jnp.concatenate([b_igate, b_fgate], axis=1), ((0, 0), (0, GATE_LANES - n_gates)))[:, None, :]
    w_pool_b = w_pool.astype(BF16)
    w_fd = w_ffn_down.astype(BF16)
    reps = CHUNK // dec_seq
    ws_sample = jnp.tile(w_s[:, :, :dec_seq, :dec_seq], (1, 1, reps, reps))
    bias_prompt = jnp.swapaxes(b_s, 1, 2)
    bias_sample = jnp.tile(jnp.swapaxes(b_s[:, :, :dec_seq], 1, 2), (1, reps, 1))
    vec = lambda a: a[:, None, :]
    g_m3, g_v3, scale3, g_mix3, g_ffn3, g_ple3 = map(vec, (g_mlstm, g_v, pool_scale, g_mix, g_ffn, g_ple))
    g_fin3 = g_final[None, None, :]

    def next_mix(i):
        return (g_mix3, i + 1) if i + 1 < depth else (g_fin3, 0)

    pp = p_prompt.reshape(depth, batch * seq, -1)
    ps = p_sample.reshape(depth, dec_batch * dec_seq, -1)
    hist_tm = jnp.swapaxes(state_pool, 1, 2)
    n_rows = jnp.repeat(state_mlstm_n.reshape(depth, dec_batch, -1), dec_seq, axis=1)
    m_rows = jnp.repeat(state_mlstm_m, dec_seq, axis=1)

    xp = x_prompt.reshape(batch * seq, d)
    xs = x_sample.reshape(dec_batch * dec_seq, d)
    xpb, ssp = _prep_rows(xp, g_mix3, 0)
    xsb, sss = _prep_rows(xs, g_mix3, 0)
    v_start = ((seq - 1) // CHUNK) * CHUNK
    outs = {k: [] for k in ("pool_p", "pool_s", "c_p", "n_p", "n_s", "m_p", "m_s", "v_p", "v_s")}
    c_s = None

    for i in range(depth):
        zp, zgp = _inproj(xpb, ssp, w_in_t, wt_gate, off_gate, n_gates, i)
        ya = _pool_prompt(zp, seq, w_pool_b, scale3, i)
        yb, c1, n1, m1 = _mlstm_prompt(zp, zgp, gate_bias, g_m3, batch, seq, q_blk, i)
        yc, vn = _chunk_mlp(zp, u_blk, w_s, bias_prompt, g_v3, seq, i)
        xp, xpb, ssp = _wout(ya, yb, yc, w_out, xp, i, g_ffn3, i)
        xp, xpb, ssp = _ffn_down(_ffn_up(xpb, ssp, w_ffn_gate, w_ffn_up, i), w_fd, xp, i, g_ple3, i)
        xp, xpb, ssp = _ple(xpb, ssp, w_ple_gate, pp, w_ple_proj, xp, i, *next_mix(i))
        outs["pool_p"].append(zp.reshape(batch, seq, -1)[:, seq - POOL_BUF:, :wa])
        outs["c_p"].append(c1)
        outs["n_p"].append(n1)
        outs["m_p"].append(m1.reshape(batch, MLSTM_HEADS))
        outs["v_p"].append(vn.reshape(batch, seq - v_start, wc))
        zs, zgs = _inproj(xsb, sss, w_in_t, wt_gate, off_gate, n_gates, i)
        new_tm = jnp.swapaxes(zs[:, :wa].reshape(dec_batch, dec_seq, wa), 0, 1)
        ya_tm = _pool_sample(hist_tm, new_tm, w_pool_b, scale3, PAST_LEN, i)
        ya = jnp.swapaxes(ya_tm, 0, 1).reshape(dec_batch * dec_seq, wa)
        yb, c_s, n1, m1 = _mlstm_sample(zs, zgs, gate_bias, g_m3, state_mlstm_c, state_mlstm_n, n_rows, m_rows,
                                        c_s, dec_seq, q_blk, i)
        yc, vn = _chunk_mlp(zs, u_blk, ws_sample, bias_sample, g_v3, dec_seq, i)
        xs, xsb, sss = _wout(ya, yb, yc, w_out, xs, i, g_ffn3, i)
        xs, xsb, sss = _ffn_down(_ffn_up(xsb, sss, w_ffn_gate, w_ffn_up, i), w_fd, xs, i, g_ple3, i)
        xs, xsb, sss = _ple(xsb, sss, w_ple_gate, ps, w_ple_proj, xs, i, *next_mix(i))
        outs["pool_s"].append(jnp.concatenate([hist_tm[i], new_tm], axis=0)[dec_seq:])
        outs["n_s"].append(n1)
        outs["m_s"].append(m1)
        outs["v_s"].append(vn.reshape(dec_batch, dec_seq, wc))

    y_prompt = _final_norm(xp, g_final[None, :]).reshape(batch, seq, d)
    y_sample = _final_norm(xs, g_final[None, :]).reshape(dec_batch, dec_seq, d)
    st = lambda k: jnp.stack(outs[k])
    pool_sample = jnp.swapaxes(st("pool_s"), 1, 2)
    return (y_prompt, y_sample, st("pool_p"), pool_sample, st("c_p"), c_s, st("n_p"), st("n_s"),
            st("m_p"), st("m_s"), st("v_p"), st("v_s"))
```

```python
import functools
import math

import jax
import jax.numpy as jnp
from jax import lax
from jax.experimental import pallas as pl
from jax.experimental.pallas import tpu as pltpu

F32 = jnp.float32
BF16 = jnp.bfloat16

EPS = 1e-6
PAST_LEN = 16384
POOL_WINDOWS = (2, 4, 8, 16)
POOL_BUF = max(POOL_WINDOWS) - 1
POOL_HALO = 16
N_POOL = len(POOL_WINDOWS)
MLSTM_HEADS = 4
N_CM = 4
CHUNK = 128
GATE_LANES = 128
SS_LANES = 128

VMEM_SLACK_BYTES = 10 * 2**20
VMEM_CAP_BYTES = 58 * 2**20

_ROW_TILES = (1024, 512, 256, 128, 64, 32, 16, 8)
_TALL_ROW_TILES = (2048,) + _ROW_TILES
_STREAM_ROWS = (512, 256, 128, 64, 32, 16, 8)


def _params(sem, vmem_estimate):
    limit = min(int(vmem_estimate) + VMEM_SLACK_BYTES, VMEM_CAP_BYTES)
    return pltpu.CompilerParams(dimension_semantics=sem, vmem_limit_bytes=limit)


def _pick(n, candidates):
    for c in candidates:
        if n % c == 0:
            return c
    return n


def _single(shape, index_map):
    return pl.BlockSpec(shape, index_map, pipeline_mode=pl.Buffered(1))


def _layer_spec(arr, layer):
    nd = arr.ndim - 1
    return pl.BlockSpec((None,) + arr.shape[1:], lambda *_: (layer,) + (0,) * nd)


def _dot(a, b):
    return jnp.dot(a, b, preferred_element_type=F32)


def _dot_nt(a, bt):
    return lax.dot_general(a, bt, (((1,), (1,)), ((), ())), preferred_element_type=F32)


def _gelu_tanh(x):
    return 0.5 * x * (1.0 + jnp.tanh(math.sqrt(2.0 / math.pi) * (x + 0.044715 * (x * x * x))))


def _seq_of(idx, seq_rows):
    if seq_rows & (seq_rows - 1) == 0:
        return lax.shift_right_logical(idx, seq_rows.bit_length() - 1)
    return idx // seq_rows


def _log_sigmoid(x):
    return jnp.minimum(x, 0.0) - jnp.log(1.0 + jnp.exp(-jnp.abs(x)))


def _row_scale(ss_ref, width):
    return lax.rsqrt(ss_ref[:, 0:1] * (1.0 / width) + EPS)


def _gain_spec(g, g_idx):
    return pl.BlockSpec((None,) + g.shape[1:], lambda i, j: (g_idx, 0, 0))


MXU_COLS = 256


def _col_groups(bn):
    w = MXU_COLS if bn % MXU_COLS == 0 else bn
    return [slice(c, c + w) for c in range(0, bn, w)]


def _group_specs(n_rows, bn, layer):
    groups = _col_groups(bn)
    gw = bn // len(groups)
    return [pl.BlockSpec((None, n_rows, gw), lambda i, j, g=g: (layer, 0, j * len(groups) + g))
            for g in range(len(groups))]


def _emit_rows(new_cols, j, gn_ref, o_ref, xb_ref, ss_ref):
    bn = o_ref.shape[1]
    sq = None
    for grp, cols in enumerate(_col_groups(bn)):
        xn = new_cols(grp, cols)
        o_ref[:, cols] = xn
        g = gn_ref[:, pl.ds(pl.multiple_of(j * bn + cols.start, cols.stop - cols.start), cols.stop - cols.start)]
        xb_ref[:, cols] = (xn * g).astype(xb_ref.dtype)
        s = jnp.sum(xn * xn, axis=-1, keepdims=True)
        sq = s if sq is None else sq + s
    part = jnp.broadcast_to(sq, ss_ref.shape)

    @pl.when(j == 0)
    def _():
        ss_ref[...] = part

    @pl.when(j > 0)
    def _():
        ss_ref[...] += part


def _row_outs(m, d, bm, bn):
    specs = [
        pl.BlockSpec((bm, bn), lambda i, j: (i, j)),
        pl.BlockSpec((bm, bn), lambda i, j: (i, j)),
        pl.BlockSpec((bm, SS_LANES), lambda i, j: (i, 0)),
    ]
    shapes = [jax.ShapeDtypeStruct((m, d), F32), jax.ShapeDtypeStruct((m, d), BF16),
              jax.ShapeDtypeStruct((m, SS_LANES), F32)]
    return specs, shapes


def _prep_rows_kernel(x_ref, gn_ref, xb_ref, ss_ref):
    x = x_ref[...]
    xb_ref[...] = (x * gn_ref[...]).astype(xb_ref.dtype)
    ss_ref[...] = jnp.broadcast_to(jnp.sum(x * x, axis=-1, keepdims=True), ss_ref.shape)


def _prep_rows(x, g, g_idx):
    m, d = x.shape
    bm = _pick(m, _STREAM_ROWS)
    return pl.pallas_call(
        _prep_rows_kernel,
        grid=(m // bm,),
        in_specs=[pl.BlockSpec((bm, d), lambda i: (i, 0)), _layer_spec(g, g_idx)],
        out_specs=[pl.BlockSpec((bm, d), lambda i: (i, 0)), pl.BlockSpec((bm, SS_LANES), lambda i: (i, 0))],
        out_shape=[jax.ShapeDtypeStruct((m, d), BF16), jax.ShapeDtypeStruct((m, SS_LANES), F32)],
        compiler_params=_params(("parallel",), 2 * bm * d * 6),
    )(x, g)


def _inproj_kernel(xb_ref, ss_ref, *refs):
    wgt_ref, z_ref, zg_ref = refs[-3:]
    wt_refs = refs[:-3]
    rs = _row_scale(ss_ref, xb_ref.shape[1])

    @pl.when(pl.program_id(1) == 0)
    def _():
        zg_ref[...] = rs * _dot_nt(xb_ref[...], wgt_ref[...].astype(BF16))

    for grp, cols in enumerate(_col_groups(z_ref.shape[1])):
        z_ref[:, cols] = rs * _dot_nt(xb_ref[...], wt_refs[grp][0].astype(BF16))


def _inproj(xb, ss, wt, wt_gate, n_lead, n_skip, layer):
    m, d = xb.shape
    n = wt.shape[1] - n_skip
    bm = _pick(m, _TALL_ROW_TILES)
    bn = _pick(math.gcd(n_lead, n), (512, 256, 128))
    lead_blocks = n_lead // bn
    assert n_skip % 8 == 0 and n_lead % bn == 0 and n % bn == 0
    est = (bm * d * 2 + 2 * d * bn * 4 + 2 * d * GATE_LANES * 4 + 2 * bm * bn * 4 + 4 * bm * GATE_LANES * 4
           + d * bn * 2)
    groups = _col_groups(bn)
    gw = bn // len(groups)

    def rows_of(grp):
        def index(i, j):
            row = j * bn + grp * gw + jnp.where(j >= lead_blocks, n_skip, 0)
            return (layer, pl.multiple_of(row, 8), 0)
        return pl.BlockSpec((pl.Element(1), pl.Element(gw), pl.Element(d)), index)

    return pl.pallas_call(
        _inproj_kernel,
        grid=(m // bm, n // bn),
        in_specs=[
            _single((bm, d), lambda i, j: (i, 0)),
            pl.BlockSpec((bm, SS_LANES), lambda i, j: (i, 0)),
            *[rows_of(grp) for grp in range(len(groups))],
            _layer_spec(wt_gate, layer),
        ],
        out_specs=[
            pl.BlockSpec((bm, bn), lambda i, j: (i, j)),
            pl.BlockSpec((bm, GATE_LANES), lambda i, j: (i, 0)),
        ],
        out_shape=[jax.ShapeDtypeStruct((m, n), F32), jax.ShapeDtypeStruct((m, GATE_LANES), F32)],
        compiler_params=_params(("parallel", "arbitrary"), est),
    )(xb, ss, *[wt] * len(groups), wt_gate)


def _ffn_up_kernel(xb_ref, ss_ref, wg_ref, wu_ref, o_ref):
    rs = _row_scale(ss_ref, xb_ref.shape[1])
    h = xb_ref[...]
    a = rs * _dot(h, wg_ref[...].astype(BF16))
    b = rs * _dot(h, wu_ref[...].astype(BF16))
    o_ref[...] = ((a * jax.nn.sigmoid(a)) * b).astype(o_ref.dtype)


def _ffn_up(xb, ss, w_gate, w_up, layer):
    m, d = xb.shape
    n = w_gate.shape[2]
    bm = _pick(m, _ROW_TILES)
    tb = _pick(n, (256, 128))
    est = 2 * bm * d * 2 + 4 * d * tb * 4 + 2 * d * tb * 2 + 2 * bm * tb * 2 + 2 * bm * SS_LANES * 4
    wspec = pl.BlockSpec((None, d, tb), lambda i, j: (layer, 0, j))
    return pl.pallas_call(
        _ffn_up_kernel,
        grid=(m // bm, n // tb),
        in_specs=[pl.BlockSpec((bm, d), lambda i, j: (i, 0)), pl.BlockSpec((bm, SS_LANES), lambda i, j: (i, 0)),
                  wspec, wspec],
        out_specs=pl.BlockSpec((bm, tb), lambda i, j: (i, j)),
        out_shape=jax.ShapeDtypeStruct((m, n), BF16),
        compiler_params=_params(("parallel", "arbitrary"), est),
    )(xb, ss, w_gate, w_up)


def _ple_kernel(xb_ref, ss_ref, *refs):
    p_ref, wpp_ref, x_ref, gn_ref, o_ref, xbo_ref, sso_ref = refs[-7:]
    w_refs = refs[:-7]
    rs = _row_scale(ss_ref, xb_ref.shape[1])
    pb = p_ref[...].astype(BF16)

    def new_cols(grp, cols):
        gate = jax.nn.sigmoid(rs * _dot(xb_ref[...], w_refs[grp][...].astype(BF16)))
        emb = _dot(pb, wpp_ref[:, cols].astype(BF16))
        return x_ref[:, cols] + gate * emb

    _emit_rows(new_cols, pl.program_id(1), gn_ref, o_ref, xbo_ref, sso_ref)


def _ple(xb, ss, w_pg, p, w_pp, x, layer, g_next, g_idx):
    m, d = x.shape
    pd = p.shape[2]
    bm = _pick(m, _ROW_TILES)
    bn = _pick(d, (512, 256, 128))
    est = (2 * bm * d * 2 + 2 * d * bn * 4 + d * bn * 2 + 2 * bm * pd * 4 + 2 * pd * bn * 4 + 4 * bm * bn * 4
           + 2 * bm * bn * 2 + 4 * bm * SS_LANES * 4)
    out_specs, out_shape = _row_outs(m, d, bm, bn)
    w_specs = _group_specs(d, bn, layer)
    return pl.pallas_call(
        _ple_kernel,
        grid=(m // bm, d // bn),
        in_specs=[
            pl.BlockSpec((bm, d), lambda i, j: (i, 0)),
            pl.BlockSpec((bm, SS_LANES), lambda i, j: (i, 0)),
            *w_specs,
            pl.BlockSpec((None, bm, pd), lambda i, j: (layer, i, 0)),
            pl.BlockSpec((None, pd, bn), lambda i, j: (layer, 0, j)),
            pl.BlockSpec((bm, bn), lambda i, j: (i, j)),
            _gain_spec(g_next, g_idx),
        ],
        out_specs=out_specs,
        out_shape=out_shape,
        compiler_params=_params(("parallel", "arbitrary"), est),
    )(xb, ss, *[w_pg] * len(w_specs), p, w_pp, x, g_next)


def _wout_kernel(ya_ref, yb_ref, yc_ref, *refs):
    x_ref, gn_ref, o_ref, xbo_ref, sso_ref = refs[-5:]
    w_refs = refs[:-5]
    wa = ya_ref.shape[1]
    wb = yb_ref.shape[1]
    ya = ya_ref[...].astype(BF16)

    def new_cols(grp, cols):
        w_ref = w_refs[grp]
        xn = x_ref[:, cols] + _dot(ya, w_ref[0:wa, :].astype(BF16))
        xn = xn + _dot(yb_ref[...], w_ref[wa:wa + wb, :].astype(BF16))
        return xn + _dot(yc_ref[...], w_ref[wa + wb:, :].astype(BF16))

    _emit_rows(new_cols, pl.program_id(1), gn_ref, o_ref, xbo_ref, sso_ref)


def _wout(ya, yb, yc, w, x, layer, g_next, g_idx):
    m, d = x.shape
    wa, wb, wc = ya.shape[1], yb.shape[1], yc.shape[1]
    k = wa + wb + wc
    bm = _pick(m, _ROW_TILES)
    bn = _pick(d, (512, 256, 128))
    est = (2 * bm * wa * ya.dtype.itemsize + 2 * bm * wb * 2 + 2 * bm * wc * 2
           + 2 * k * bn * 4 + k * bn * 2 + 4 * bm * bn * 4 + 2 * bm * bn * 2 + 2 * bm * SS_LANES * 4)
    out_specs, out_shape = _row_outs(m, d, bm, bn)
    w_specs = _group_specs(k, bn, layer)
    return pl.pallas_call(
        _wout_kernel,
        grid=(m // bm, d // bn),
        in_specs=[
            pl.BlockSpec((bm, wa), lambda i, j: (i, 0)),
            pl.BlockSpec((bm, wb), lambda i, j: (i, 0)),
            pl.BlockSpec((bm, wc), lambda i, j: (i, 0)),
            *w_specs,
            pl.BlockSpec((bm, bn), lambda i, j: (i, j)),
            _gain_spec(g_next, g_idx),
        ],
        out_specs=out_specs,
        out_shape=out_shape,
        compiler_params=_params(("parallel", "arbitrary"), est),
    )(ya, yb, yc, *[w] * len(w_specs), x, g_next)


def _ffn_down_kernel(a_ref, *refs):
    x_ref, gn_ref, o_ref, xbo_ref, sso_ref = refs[-5:]
    w_refs = refs[:-5]
    _emit_rows(lambda grp, cols: x_ref[:, cols] + _dot(a_ref[...], w_refs[grp][...]),
               pl.program_id(1), gn_ref, o_ref, xbo_ref, sso_ref)


def _ffn_down(a, w, x, layer, g_next, g_idx):
    m, d = x.shape
    k = a.shape[1]
    bm = _pick(m, _ROW_TILES[1:])
    bn = _pick(d, (512, 256, 128))
    est = 2 * bm * k * 2 + 2 * k * bn * 2 + 4 * bm * bn * 4 + 2 * bm * bn * 2 + 2 * bm * SS_LANES * 4
    out_specs, out_shape = _row_outs(m, d, bm, bn)
    w_specs = _group_specs(k, bn, layer)
    return pl.pallas_call(
        _ffn_down_kernel,
        grid=(m // bm, d // bn),
        in_specs=[
            pl.BlockSpec((bm, k), lambda i, j: (i, 0)),
            *w_specs,
            pl.BlockSpec((bm, bn), lambda i, j: (i, j)),
            _gain_spec(g_next, g_idx),
        ],
        out_specs=out_specs,
        out_shape=out_shape,
        compiler_params=_params(("parallel", "arbitrary"), est),
    )(a, *[w] * len(w_specs), x, g_next)


def _final_norm_kernel(x_ref, g_ref, o_ref):
    x = x_ref[...]
    ms = jnp.mean(x * x, axis=-1, keepdims=True)
    o_ref[...] = (x * lax.rsqrt(ms + EPS)) * g_ref[...]


def _final_norm(x, g):
    m, d = x.shape
    bm = _pick(m, _STREAM_ROWS)
    return pl.pallas_call(
        _final_norm_kernel,
        grid=(m // bm,),
        in_specs=[pl.BlockSpec((bm, d), lambda i: (i, 0)), pl.BlockSpec((1, d), lambda i: (0, 0))],
        out_specs=pl.BlockSpec((bm, d), lambda i: (i, 0)),
        out_shape=jax.ShapeDtypeStruct((m, d), F32),
        compiler_params=_params(("parallel",), 4 * bm * d * 4),
    )(x, g)


def _pool_prompt_kernel(a_ref, halo_ref, wp_ref, sc_ref, y_ref, ext_scr, *, tiles_per_seq):
    tp = a_ref.shape[0]
    cg = wp_ref.shape[1]
    tile = pl.program_id(0) % tiles_per_seq
    ext_scr[0:POOL_HALO, :] = jnp.where(tile == 0, 0.0, halo_ref[...])
    ext_scr[POOL_HALO:, :] = a_ref[...]
    pos = tile * tp + lax.broadcasted_iota(jnp.int32, (tp, 1), 0)
    for g, w in enumerate(POOL_WINDOWS):
        sl = slice(g * cg, (g + 1) * cg)
        s = ext_scr[POOL_HALO:POOL_HALO + tp, sl]
        for k in range(1, w):
            s = s + ext_scr[POOL_HALO - k:POOL_HALO - k + tp, sl]
        cnt = jnp.minimum(pos + 1, w).astype(F32)
        dlt = s / cnt - a_ref[:, sl]
        y = _dot(dlt.astype(BF16), wp_ref[g]) * sc_ref[:, sl]
        y_ref[:, sl] = y.astype(y_ref.dtype)


def _pool_prompt(z, seq, w_pool, scale, layer):
    m = z.shape[0]
    wa = scale.shape[2]
    tp = _pick(seq, (512, 256, 128, 64, 32, 16))
    tiles_per_seq = seq // tp
    halo_blocks = tp // POOL_HALO
    est = (2 * tp * wa * 4 + 2 * POOL_HALO * wa * 4 + (tp + POOL_HALO) * wa * 4 + 2 * tp * wa * 2
           + 4 * wa * wa // N_POOL)
    return pl.pallas_call(
        functools.partial(_pool_prompt_kernel, tiles_per_seq=tiles_per_seq),
        grid=(m // tp,),
        in_specs=[
            pl.BlockSpec((tp, wa), lambda i: (i, 0)),
            pl.BlockSpec((POOL_HALO, wa), lambda i: (jnp.maximum(i * halo_blocks - 1, 0), 0)),
            _layer_spec(w_pool, layer),
            _layer_spec(scale, layer),
        ],
        out_specs=pl.BlockSpec((tp, wa), lambda i: (i, 0)),
        out_shape=jax.ShapeDtypeStruct((m, wa), BF16),
        scratch_shapes=[pltpu.VMEM((tp + POOL_HALO, wa), F32)],
        compiler_params=_params(("parallel",), est),
    )(z, z, w_pool, scale)


def _pool_sample_kernel(hist_ref, new_ref, wp_ref, sc_ref, y_ref, *, first_pos):
    cg = wp_ref.shape[1]
    n_new = new_ref.shape[0]
    rows = [hist_ref[r] for r in range(POOL_BUF)] + [new_ref[t] for t in range(n_new)]
    for t in range(n_new):
        for g, w in enumerate(POOL_WINDOWS):
            sl = slice(g * cg, (g + 1) * cg)
            s = rows[POOL_BUF + t][:, sl]
            for k in range(1, w):
                s = s + rows[POOL_BUF + t - k][:, sl]
            cnt = float(min(first_pos + t + 1, w))
            dlt = s / cnt - rows[POOL_BUF + t][:, sl]
            y = _dot(dlt.astype(BF16), wp_ref[g]) * sc_ref[:, sl]
            y_ref[t, :, sl] = y.astype(y_ref.dtype)


def _pool_sample(hist, new, w_pool, scale, first_pos, layer):
    n_new, b, wa = new.shape
    bb = _pick(b, (32, 16, 8))
    est = 2 * (POOL_BUF + 2 * n_new) * bb * wa * 4 + 4 * wa * wa // N_POOL
    return pl.pallas_call(
        functools.partial(_pool_sample_kernel, first_pos=first_pos),
        grid=(b // bb,),
        in_specs=[
            pl.BlockSpec((None, POOL_BUF, bb, wa), lambda i: (layer, 0, i, 0)),
            pl.BlockSpec((n_new, bb, wa), lambda i: (0, i, 0)),
            _layer_spec(w_pool, layer),
            _layer_spec(scale, layer),
        ],
        out_specs=pl.BlockSpec((n_new, bb, wa), lambda i: (0, i, 0)),
        out_shape=jax.ShapeDtypeStruct((n_new, b, wa), F32),
        compiler_params=_params(("parallel",), est),
    )(hist, new, w_pool, scale)


def _chunk_mlp_kernel(u_ref, v_ref, ws_ref, b_ref, gv_ref, yc_ref, vn_ref, *, seq_rows):
    rows = u_ref.shape[0]
    cg = u_ref.shape[1] // N_CM
    ri = lax.broadcasted_iota(jnp.int32, (rows, rows), 0)
    ci = lax.broadcasted_iota(jnp.int32, (rows, rows), 1)
    mask = ci <= ri
    if seq_rows < rows:
        mask = mask & (_seq_of(ri, seq_rows) == _seq_of(ci, seq_rows))
    for g in range(N_CM):
        sl = slice(g * cg, (g + 1) * cg)
        u = _gelu_tanh(u_ref[:, sl])
        v = _gelu_tanh(v_ref[:, sl])
        ms = jnp.mean(v * v, axis=-1, keepdims=True)
        vn = (v * lax.rsqrt(ms + EPS)) * gv_ref[:, sl]
        vn_ref[:, sl] = vn
        wm = jnp.where(mask, ws_ref[g], 0.0).astype(BF16)
        mix = _dot(wm, vn.astype(BF16)) + b_ref[:, g:g + 1]
        yc_ref[:, sl] = (u * mix).astype(yc_ref.dtype)


def _chunk_mlp(z, u_blk, w_mix, bias, g_v, seq_rows, layer):
    m = z.shape[0]
    wc = g_v.shape[2]
    tiles_per_seq = max(seq_rows // CHUNK, 1)
    seq_rows = min(seq_rows, CHUNK)
    assert m % (tiles_per_seq * CHUNK) == 0
    est = 4 * CHUNK * wc * 4 + 2 * N_CM * CHUNK * CHUNK * 4 + 2 * CHUNK * wc * 2 + 2 * CHUNK * wc * 4
    return pl.pallas_call(
        functools.partial(_chunk_mlp_kernel, seq_rows=seq_rows),
        grid=(m // CHUNK,),
        in_specs=[
            pl.BlockSpec((CHUNK, wc), lambda i: (i, u_blk)),
            pl.BlockSpec((CHUNK, wc), lambda i: (i, u_blk + 1)),
            _layer_spec(w_mix, layer),
            _layer_spec(bias, layer),
            _layer_spec(g_v, layer),
        ],
        out_specs=[pl.BlockSpec((CHUNK, wc), lambda i: (i, 0)),
                   pl.BlockSpec((CHUNK, wc), lambda i: (i // tiles_per_seq, 0))],
        out_shape=[jax.ShapeDtypeStruct((m, wc), BF16), jax.ShapeDtypeStruct((m // tiles_per_seq, wc), F32)],
        compiler_params=_params(("arbitrary",), est),
    )(z, z, w_mix, bias, g_v)


def _col_to_row(col, eye):
    return jnp.sum(jnp.where(eye, col, 0.0), axis=0, keepdims=True)


def _head_blocks(refs, h, width):
    r = refs[h // 2]
    return r[:, (h % 2) * width:(h % 2 + 1) * width]


def _mlstm_gates(zg_ref, bias_ref, h, same, causal, eye):
    i_col = zg_ref[:, h:h + 1] + bias_ref[:, h:h + 1]
    f_col = zg_ref[:, MLSTM_HEADS + h:MLSTM_HEADS + h + 1] + bias_ref[:, MLSTM_HEADS + h:MLSTM_HEADS + h + 1]
    lf_col = _log_sigmoid(f_col)
    lf_row = _col_to_row(lf_col, eye)
    i_row = _col_to_row(i_col, eye)
    fcum_col = jnp.sum(jnp.where(causal, lf_row, 0.0), axis=1, keepdims=True)
    fcum_row = _col_to_row(fcum_col, eye)
    ftot_col = jnp.sum(jnp.where(same, lf_row, 0.0), axis=1, keepdims=True)
    return i_col, i_row, fcum_col, fcum_row, ftot_col


def _mlstm_prompt_kernel(q_ref, k_ref, v0_ref, v1_ref, o0_ref, o1_ref, zg_ref, bias_ref, gm_ref,
                         yb_ref, c_ref, n_ref, m_ref, c_scr, n_scr, m_scr):
    L = q_ref.shape[0]
    dk = q_ref.shape[1] // MLSTM_HEADS
    dv = 2 * v0_ref.shape[1] // MLSTM_HEADS
    step = pl.program_id(1)

    @pl.when(step == 0)
    def _():
        c_scr[...] = jnp.zeros_like(c_scr)
        n_scr[...] = jnp.zeros_like(n_scr)
        m_scr[...] = jnp.zeros_like(m_scr)

    ri = lax.broadcasted_iota(jnp.int32, (L, L), 0)
    ci = lax.broadcasted_iota(jnp.int32, (L, L), 1)
    causal = ci <= ri
    eye = ci == ri
    same = ci >= 0
    for h in range(MLSTM_HEADS):
        q = q_ref[:, h * dk:(h + 1) * dk] * (dk ** -0.5)
        k = k_ref[:, h * dk:(h + 1) * dk]
        v = _head_blocks((v0_ref, v1_ref), h, dv)
        o = _head_blocks((o0_ref, o1_ref), h, dv)
        i_col, i_row, fcum_col, fcum_row, ftot = _mlstm_gates(zg_ref, bias_ref, h, same, causal, eye)
        m_prev = m_scr[h:h + 1, 0:1]
        dlog = jnp.where(causal, fcum_col - fcum_row + i_row, -jnp.inf)
        m_inter = m_prev + fcum_col
        m_tok = jnp.maximum(m_inter, jnp.max(dlog, axis=1, keepdims=True))
        qb = q.astype(BF16)
        kb = k.astype(BF16)
        vb = v.astype(BF16)
        s = _dot_nt(qb, kb) * jnp.exp(dlog - m_tok)
        w_inter = jnp.exp(m_inter - m_tok)
        cmat = c_scr[h]
        nvec = n_scr[h:h + 1, :]
        num = _dot(s.astype(BF16), vb) + w_inter * _dot(qb, cmat.astype(BF16))
        den = jnp.sum(s, axis=1, keepdims=True) + w_inter * jnp.sum(q * nvec, axis=1, keepdims=True)
        den = jnp.maximum(jnp.abs(den), jnp.exp(-m_tok))
        hh = num / den
        ms = jnp.mean(hh * hh, axis=-1, keepdims=True)
        yn = (hh * lax.rsqrt(ms + EPS)) * gm_ref[:, h * dv:(h + 1) * dv]
        yb_ref[:, h * dv:(h + 1) * dv] = (yn * jax.nn.sigmoid(o)).astype(yb_ref.dtype)
        wlog = ftot - fcum_col + i_col
        m_new = jnp.maximum(m_prev + ftot[0:1, :], jnp.max(wlog, axis=0, keepdims=True))
        wk = jnp.exp(wlog - m_new)
        decay = jnp.exp(m_prev + ftot[0:1, :] - m_new)
        kw = k * wk
        c_scr[h] = decay * cmat + lax.dot_general(kw.astype(BF16), vb, (((0,), (0,)), ((), ())),
                                                  preferred_element_type=F32)
        n_scr[h:h + 1, :] = decay * nvec + jnp.sum(kw, axis=0, keepdims=True)
        m_scr[h:h + 1, :] = jnp.broadcast_to(m_new, (1, m_scr.shape[1]))

    @pl.when(step == pl.num_programs(1) - 1)
    def _():
        c_ref[0] = c_scr[...]
        n_ref[0] = n_scr[...]
        m_ref[0] = m_scr[0:MLSTM_HEADS, 0:1]


def _mlstm_prompt(z, zg, bias, g_m, batch, seq, q_blk, layer):
    m = z.shape[0]
    wb = g_m.shape[2]
    qw = wb // 2
    dk = qw // MLSTM_HEADS
    dv = wb // MLSTM_HEADS
    L = _pick(seq, (256, 128, 64, 32, 16, 8))
    nc = seq // L
    row = lambda b, c: b * nc + c
    zspec = lambda blk: pl.BlockSpec((L, qw), lambda b, c: (row(b, c), blk))
    est = 2 * 6 * L * qw * 4 + 2 * L * GATE_LANES * 4 + 2 * L * wb * 2 + 3 * MLSTM_HEADS * dk * dv * 4 + 16 * L * L * 4
    return pl.pallas_call(
        _mlstm_prompt_kernel,
        grid=(batch, nc),
        in_specs=[zspec(q_blk + i) for i in range(6)] + [
            pl.BlockSpec((L, GATE_LANES), lambda b, c: (row(b, c), 0)),
            _layer_spec(bias, layer),
            _layer_spec(g_m, layer),
        ],
        out_specs=[
            pl.BlockSpec((L, wb), lambda b, c: (row(b, c), 0)),
            pl.BlockSpec((1, MLSTM_HEADS, dk, dv), lambda b, c: (b, 0, 0, 0)),
            pl.BlockSpec((1, MLSTM_HEADS, dk), lambda b, c: (b, 0, 0)),
            pl.BlockSpec((1, MLSTM_HEADS, 1), lambda b, c: (b, 0, 0)),
        ],
        out_shape=[
            jax.ShapeDtypeStruct((m, wb), BF16),
            jax.ShapeDtypeStruct((batch, MLSTM_HEADS, dk, dv), F32),
            jax.ShapeDtypeStruct((batch, MLSTM_HEADS, dk), F32),
            jax.ShapeDtypeStruct((batch, MLSTM_HEADS, 1), F32),
        ],
        scratch_shapes=[
            pltpu.VMEM((MLSTM_HEADS, dk, dv), F32),
            pltpu.VMEM((MLSTM_HEADS, dk), F32),
            pltpu.VMEM((8, 128), F32),
        ],
        compiler_params=_params(("parallel", "arbitrary"), est),
    )(z, z, z, z, z, z, zg, bias, g_m)


def _mlstm_sample_kernel(q_ref, k_ref, v0_ref, v1_ref, o0_ref, o1_ref, zg_ref, bias_ref, gm_ref,
                         c0_ref, nrow_ref, mrow_ref, n0_ref, *rest, seq_rows):
    yb_ref, c_ref, n_ref, m_ref = rest[-4:]
    R = q_ref.shape[0]
    nb = R // seq_rows
    dk = q_ref.shape[1] // MLSTM_HEADS
    dv = 2 * v0_ref.shape[1] // MLSTM_HEADS
    ri = lax.broadcasted_iota(jnp.int32, (R, R), 0)
    ci = lax.broadcasted_iota(jnp.int32, (R, R), 1)
    same = _seq_of(ri, seq_rows) == _seq_of(ci, seq_rows)
    causal = same & (ci <= ri)
    eye = ci == ri
    rid = lax.broadcasted_iota(jnp.int32, (R, 1), 0)
    for h in range(MLSTM_HEADS):
        q = q_ref[:, h * dk:(h + 1) * dk] * (dk ** -0.5)
        k = k_ref[:, h * dk:(h + 1) * dk]
        v = _head_blocks((v0_ref, v1_ref), h, dv)
        o = _head_blocks((o0_ref, o1_ref), h, dv)
        i_col, i_row, fcum_col, fcum_row, ftot = _mlstm_gates(zg_ref, bias_ref, h, same, causal, eye)
        m_prev = mrow_ref[:, h:h + 1]
        dlog = jnp.where(causal, fcum_col - fcum_row + i_row, -jnp.inf)
        m_inter = m_prev + fcum_col
        m_tok = jnp.maximum(m_inter, jnp.max(dlog, axis=1, keepdims=True))
        qb = q.astype(BF16)
        kb = k.astype(BF16)
        vb = v.astype(BF16)
        s = _dot_nt(qb, kb) * jnp.exp(dlog - m_tok)
        w_inter = jnp.exp(m_inter - m_tok)
        inter = jnp.zeros((R, dv), F32)
        for b in range(nb):
            mine = _seq_of(rid, seq_rows) == b
            inter = inter + jnp.where(mine, _dot(qb, c0_ref[b, h].astype(BF16)), 0.0)
        num = _dot(s.astype(BF16), vb) + w_inter * inter
        nrow = nrow_ref[:, h * dk:(h + 1) * dk]
        den = jnp.sum(s, axis=1, keepdims=True) + w_inter * jnp.sum(q * nrow, axis=1, keepdims=True)
        den = jnp.maximum(jnp.abs(den), jnp.exp(-m_tok))
        hh = num / den
        ms = jnp.mean(hh * hh, axis=-1, keepdims=True)
        yn = (hh * lax.rsqrt(ms + EPS)) * gm_ref[:, h * dv:(h + 1) * dv]
        yb_ref[:, h * dv:(h + 1) * dv] = (yn * jax.nn.sigmoid(o)).astype(yb_ref.dtype)
        wlog_col = ftot - fcum_col + i_col
        wlog_row = _col_to_row(wlog_col, eye)
        m_new = jnp.maximum(m_prev + ftot, jnp.max(jnp.where(same, wlog_row, -jnp.inf), axis=1, keepdims=True))
        wk = jnp.exp(wlog_col - m_new)
        decay = jnp.exp(m_prev + ftot - m_new)
        kw = k * wk
        for b in range(nb):
            mine = _seq_of(rid, seq_rows) == b
            kw_b = jnp.where(mine, kw, 0.0)
            r0 = b * seq_rows
            dec_b = decay[r0:r0 + 1, :]
            upd = lax.dot_general(kw_b.astype(BF16), vb, (((0,), (0,)), ((), ())), preferred_element_type=F32)
            c_ref[b, h] = dec_b * c0_ref[b, h] + upd
            n_ref[b, h:h + 1, :] = dec_b * n0_ref[b, h:h + 1, :] + jnp.sum(kw_b, axis=0, keepdims=True)
            m_ref[0, b:b + 1, h:h + 1] = m_new[r0:r0 + 1, :]


def _mlstm_sample(z, zg, bias, g_m, c0, n0, n_rows, m_rows, c_acc, seq_rows, q_blk, layer):
    m = z.shape[0]
    depth, batch = c0.shape[:2]
    wb = g_m.shape[2]
    qw = wb // 2
    dk = qw // MLSTM_HEADS
    dv = wb // MLSTM_HEADS
    nb = 4
    R = nb * seq_rows
    zspec = lambda blk: pl.BlockSpec((R, qw), lambda s: (s, blk))
    cblk = nb * MLSTM_HEADS * dk * dv * 4
    est = 4 * cblk + 2 * 7 * R * qw * 4 + 2 * R * GATE_LANES * 4 + 2 * R * wb * 2
    in_specs = [zspec(q_blk + i) for i in range(6)] + [
        pl.BlockSpec((R, GATE_LANES), lambda s: (s, 0)),
        _layer_spec(bias, layer),
        _layer_spec(g_m, layer),
        pl.BlockSpec((None, nb, MLSTM_HEADS, dk, dv), lambda s: (layer, s, 0, 0, 0)),
        pl.BlockSpec((None, R, qw), lambda s: (layer, s, 0)),
        pl.BlockSpec((None, R, MLSTM_HEADS), lambda s: (layer, s, 0)),
        pl.BlockSpec((None, nb, MLSTM_HEADS, dk), lambda s: (layer, s, 0, 0)),
    ]
    args = [z, z, z, z, z, z, zg, bias, g_m, c0, n_rows, m_rows, n0]
    aliases = {}
    if c_acc is not None:
        in_specs.append(pl.BlockSpec(memory_space=pl.ANY))
        aliases = {len(args): 1}
        args.append(c_acc)
    yb, c_all, n1, m1 = pl.pallas_call(
        functools.partial(_mlstm_sample_kernel, seq_rows=seq_rows),
        grid=(m // R,),
        in_specs=in_specs,
        out_specs=[
            pl.BlockSpec((R, wb), lambda s: (s, 0)),
            pl.BlockSpec((None, nb, MLSTM_HEADS, dk, dv), lambda s: (layer, s, 0, 0, 0)),
            pl.BlockSpec((nb, MLSTM_HEADS, dk), lambda s: (s, 0, 0)),
            pl.BlockSpec((1, nb, MLSTM_HEADS), lambda s: (s, 0, 0)),
        ],
        out_shape=[
            jax.ShapeDtypeStruct((m, wb), BF16),
            jax.ShapeDtypeStruct(c0.shape, F32),
            jax.ShapeDtypeStruct(n0.shape[1:], F32),
            jax.ShapeDtypeStruct((batch // nb, nb, MLSTM_HEADS), F32),
        ],
        input_output_aliases=aliases,
        compiler_params=_params(("parallel",), est),
    )(*args)
    return yb, c_all, n1, m1.reshape(batch, MLSTM_HEADS)


def kernel(x_prompt, x_sample, state_pool, state_mlstm_c, state_mlstm_n, state_mlstm_m, p_prompt, p_sample, g_mix, w_in, b_igate, b_fgate, w_pool, pool_scale, g_mlstm, w_s, b_s, g_v, w_out, g_ffn, w_ffn_gate, w_ffn_up, w_ffn_down, g_ple, w_ple_gate, w_ple_proj, g_final):
    batch, seq, d = x_prompt.shape
    dec_batch, dec_seq, _ = x_sample.shape
    depth = w_in.shape[0]
    wa = pool_scale.shape[1]
    wb = g_mlstm.shape[1]
    wc = g_v.shape[1]
    n_gates = 2 * MLSTM_HEADS
    off_gate = wa + 2 * (wb // 2) + 2 * wb
    off_c = off_gate + n_gates
    q_blk = wa // (wb // 2)
    u_blk = off_gate // wc
    assert seq % CHUNK == 0

    w_in_t = jnp.swapaxes(w_in, 1, 2)
    wt_gate = jnp.pad(w_in_t[:, off_gate:off_c], ((0, 0), (0, GATE_LANES - n_gates), (0, 0)))
    gate_bias = jnp.pad(jnp.concatenate([b_igate, b_fgate], axis=1), ((0, 0), (0, GATE_LANES - n_gates)))[:, None, :]
    w_pool_b = w_pool.astype(BF16)
    w_fd = w_ffn_down.astype(BF16)
    reps = CHUNK // dec_seq
    ws_sample = jnp.tile(w_s[:, :, :dec_seq, :dec_seq], (1, 1, reps, reps))
    bias_prompt = jnp.swapaxes(b_s, 1, 2)
    bias_sample = jnp.tile(jnp.swapaxes(b_s[:, :, :dec_seq], 1, 2), (1, reps, 1))
    vec = lambda a: a[:, None, :]
    g_m3, g_v3, scale3, g_mix3, g_ffn3, g_ple3 = map(vec, (g_mlstm, g_v, pool_scale, g_mix, g_ffn, g_ple))
    g_fin3 = g_final[None, None, :]

    def next_mix(i):
        return (g_mix3, i + 1) if i + 1 < depth else (g_fin3, 0)

    pp = p_prompt.reshape(depth, batch * seq, -1)
    ps = p_sample.reshape(depth, dec_batch * dec_seq, -1)
    hist_tm = jnp.swapaxes(state_pool, 1, 2)
    n_rows = jnp.repeat(state_mlstm_n.reshape(depth, dec_batch, -1), dec_seq, axis=1)
    m_rows = jnp.repeat(state_mlstm_m, dec_seq, axis=1)

    xp = x_prompt.reshape(batch * seq, d)
    xs = x_sample.reshape(dec_batch * dec_seq, d)
    xpb, ssp = _prep_rows(xp, g_mix3, 0)
    xsb, sss = _prep_rows(xs, g_mix3, 0)
    v_start = ((seq - 1) // CHUNK) * CHUNK
    outs = {k: [] for k in ("pool_p", "pool_s", "c_p", "n_p", "n_s", "m_p", "m_s", "v_p", "v_s")}
    c_s = None

    for i in range(depth):
        zp, zgp = _inproj(xpb, ssp, w_in_t, wt_gate, off_gate, n_gates, i)
        ya = _pool_prompt(zp, seq, w_pool_b, scale3, i)
        yb, c1, n1, m1 = _mlstm_prompt(zp, zgp, gate_bias, g_m3, batch, seq, q_blk, i)
        yc, vn = _chunk_mlp(zp, u_blk, w_s, bias_prompt, g_v3, seq, i)
        xp, xpb, ssp = _wout(ya, yb, yc, w_out, xp, i, g_ffn3, i)
        xp, xpb, ssp = _ffn_down(_ffn_up(xpb, ssp, w_ffn_gate, w_ffn_up, i), w_fd, xp, i, g_ple3, i)
        xp, xpb, ssp = _ple(xpb, ssp, w_ple_gate, pp, w_ple_proj, xp, i, *next_mix(i))
        outs["pool_p"].append(zp.reshape(batch, seq, -1)[:, seq - POOL_BUF:, :wa])
        outs["c_p"].append(c1)
        outs["n_p"].append(n1)
        outs["m_p"].append(m1.reshape(batch, MLSTM_HEADS))
        outs["v_p"].append(vn.reshape(batch, seq - v_start, wc))
        zs, zgs = _inproj(xsb, sss, w_in_t, wt_gate, off_gate, n_gates, i)
        new_tm = jnp.swapaxes(zs[:, :wa].reshape(dec_batch, dec_seq, wa), 0, 1)
        ya_tm = _pool_sample(hist_tm, new_tm, w_pool_b, scale3, PAST_LEN, i)
        ya = jnp.swapaxes(ya_tm, 0, 1).reshape(dec_batch * dec_seq, wa)
        yb, c_s, n1, m1 = _mlstm_sample(zs, zgs, gate_bias, g_m3, state_mlstm_c, state_mlstm_n, n_rows, m_rows,
                                        c_s, dec_seq, q_blk, i)
        yc, vn = _chunk_mlp(zs, u_blk, ws_sample, bias_sample, g_v3, dec_seq, i)
        xs, xsb, sss = _wout(ya, yb, yc, w_out, xs, i, g_ffn3, i)
        xs, xsb, sss = _ffn_down(_ffn_up(xsb, sss, w_ffn_gate, w_ffn_up, i), w_fd, xs, i, g_ple3, i)
        xs, xsb, sss = _ple(xsb, sss, w_ple_gate, ps, w_ple_proj, xs, i, *next_mix(i))
        outs["pool_s"].append(jnp.concatenate([hist_tm[i], new_tm], axis=0)[dec_seq:])
        outs["n_s"].append(n1)
        outs["m_s"].append(m1)
        outs["v_s"].append(vn.reshape(dec_batch, dec_seq, wc))

    y_prompt = _final_norm(xp, g_final[None, :]).reshape(batch, seq, d)
    y_sample = _final_norm(xs, g_final[None, :]).reshape(dec_batch, dec_seq, d)
    st = lambda k: jnp.stack(outs[k])
    pool_sample = jnp.swapaxes(st("pool_s"), 1, 2)
    return (y_prompt, y_sample, st("pool_p"), pool_sample, st("c_p"), c_s, st("n_p"), st("n_s"),
            st("m_p"), st("m_s"), st("v_p"), st("v_s"))
```
